```python
import jax, jax.numpy as jnp
from jax import lax
import numpy as np

D_MODEL = 1024
BATCH = 16
SEQ = 4096
DEPTH = 1

GLA_HEADS = 4
GLA_WIDTH = 512
GLA_DV = GLA_WIDTH // GLA_HEADS
GLA_DK = GLA_DV // 2
GLA_KEY = GLA_HEADS * GLA_DK
GATE_RANK = 16
GATE_NORMALIZER = 16.0
GLA_CHUNK = 64

SG_GROUPS = 4
SG_WIDTH = 512
SG_CH = SG_WIDTH // SG_GROUPS
SG_CHUNK = 128

MIX_WIDTH = GLA_WIDTH + SG_WIDTH

D_FF = -(-8 * D_MODEL // (3 * 256)) * 256

NORM_EPS = 1e-5

W_Q = GLA_KEY
W_K = GLA_KEY
W_V = GLA_WIDTH
W_GLR = GATE_RANK
W_GOUT = GLA_WIDTH
W_SU = SG_WIDTH
W_SV = SG_WIDTH
IN_COLS = W_Q + W_K + W_V + W_GLR + W_GOUT + W_SU + W_SV
SPLITS = (W_Q,
          W_Q + W_K,
          W_Q + W_K + W_V,
          W_Q + W_K + W_V + W_GLR,
          W_Q + W_K + W_V + W_GLR + W_GOUT,
          W_Q + W_K + W_V + W_GLR + W_GOUT + W_SU)

kernel_name = "gla_sgu_parallel_hybrid_block"


def rms_norm(x, w):
    xf = x.astype(jnp.float32)
    y = xf * lax.rsqrt(jnp.mean(xf * xf, axis=-1, keepdims=True) + NORM_EPS)
    return (y * w.astype(jnp.float32)).astype(x.dtype)


def layer_norm(x, w, b):
    xf = x.astype(jnp.float32)
    mu = jnp.mean(xf, axis=-1, keepdims=True)
    var = jnp.mean(jnp.square(xf - mu), axis=-1, keepdims=True)
    y = (xf - mu) * lax.rsqrt(var + NORM_EPS)
    return (y * w.astype(jnp.float32) + b.astype(jnp.float32)).astype(x.dtype)


def gla_chunked(q, k, v, log_a):
    B, T, H, DK = q.shape
    DV = v.shape[-1]
    C = GLA_CHUNK
    N = T // C

    def blk(t):
        return t.reshape(B, N, C, H, t.shape[-1])

    q, k, v, log_a = blk(q), blk(k), blk(v), blk(log_a)
    b = jnp.cumsum(log_a, axis=2)
    b_last = b[:, :, -1:]
    q_dec = q * jnp.exp(b)
    k_inv = k * jnp.exp(-b)
    k_end = k * jnp.exp(b_last - b)

    scores = jnp.einsum('bnihd,bnjhd->bnhij', q_dec, k_inv)
    causal = jnp.tril(jnp.ones((C, C), dtype=bool))
    scores = jnp.where(causal, scores, jnp.zeros((), scores.dtype))
    o_intra = jnp.einsum('bnhij,bnjhe->bnihe', scores, v)

    u = jnp.einsum('bnjhd,bnjhe->nbhde', k_end, v)
    decay = jnp.moveaxis(jnp.exp(b_last[:, :, 0]), 1, 0)

    def step(state, inp):
        d, un = inp
        return d[..., None] * state + un, state

    s0 = jnp.zeros((B, H, DK, DV), q.dtype)
    _, s_prev = lax.scan(step, s0, (decay, u))
    o_inter = jnp.einsum('bnihd,nbhde->bnihe', q_dec, s_prev)
    return (o_intra + o_inter).reshape(B, T, H, DV)


def spatial_gate(u, v, ln_w, ln_b, w_s, b_s):
    B, T, G, Cc = v.shape
    N = T // SG_CHUNK
    v = layer_norm(v, ln_w, ln_b).reshape(B, N, SG_CHUNK, G, Cc)
    causal = jnp.tril(jnp.ones((SG_CHUNK, SG_CHUNK), dtype=bool))
    w = jnp.where(causal, w_s, jnp.zeros((), w_s.dtype))
    mixed = jnp.einsum('gts,bnsgc->bntgc', w, v) + jnp.transpose(b_s)[None, None, :, :, None]
    return u * mixed.reshape(B, T, G, Cc)


def setup_inputs(seed: int = 0) -> dict:
    key = jax.random.key(seed)
    ks = jax.random.split(key, 20)
    f32 = jnp.float32

    def nrm(k, shape, scale):
        return jax.random.normal(k, shape, f32) * scale

    return {
        "x": jax.random.normal(ks[0], (BATCH, SEQ, D_MODEL), f32),
        "norm1_w": 1.0 + nrm(ks[1], (DEPTH, D_MODEL), 0.02),
        "w_in": nrm(ks[2], (DEPTH, D_MODEL, IN_COLS), D_MODEL ** -0.5),
        "w_gate_up": nrm(ks[3], (DEPTH, GATE_RANK, GLA_KEY), GATE_RANK ** -0.5),
        "b_gate_up": nrm(ks[4], (DEPTH, GLA_KEY), 0.1),
        "gla_norm_w": 1.0 + nrm(ks[5], (DEPTH, GLA_DV), 0.02),
        "sg_ln_w": 1.0 + nrm(ks[6], (DEPTH, SG_GROUPS, SG_CH), 0.02),
        "sg_ln_b": nrm(ks[7], (DEPTH, SG_GROUPS, SG_CH), 0.02),
        "sg_w_s": nrm(ks[8], (DEPTH, SG_GROUPS, SG_CHUNK, SG_CHUNK), SG_CHUNK ** -0.5),
        "sg_b_s": 1.0 + nrm(ks[9], (DEPTH, SG_GROUPS, SG_CHUNK), 0.02),
        "w_out": nrm(ks[10], (DEPTH, MIX_WIDTH, D_MODEL), MIX_WIDTH ** -0.5),
        "norm2_w": 1.0 + nrm(ks[11], (DEPTH, D_MODEL), 0.02),
        "w_ffn_gate": nrm(ks[12], (DEPTH, D_MODEL, D_FF), D_MODEL ** -0.5),
        "w_ffn_up": nrm(ks[13], (DEPTH, D_MODEL, D_FF), D_MODEL ** -0.5),
        "w_ffn_down": nrm(ks[14], (DEPTH, D_FF, D_MODEL), D_FF ** -0.5),
        "final_norm_w": 1.0 + nrm(ks[15], (D_MODEL,), 0.02),
    }


def reference(x, norm1_w, w_in, w_gate_up, b_gate_up, gla_norm_w, sg_ln_w, sg_ln_b,
              sg_w_s, sg_b_s, w_out, norm2_w, w_ffn_gate, w_ffn_up, w_ffn_down,
              final_norm_w):
    B, T, _ = x.shape
    f32 = jnp.float32
    h = x
    for l in range(DEPTH):
        n = rms_norm(h, norm1_w[l])
        z = jnp.einsum('btd,dc->btc', n, w_in[l])
        q, k, v, g_lr, g_out, su, sv = jnp.split(z, SPLITS, axis=-1)

        qh = q.reshape(B, T, GLA_HEADS, GLA_DK).astype(f32) * (GLA_DK ** -0.5)
        kh = k.reshape(B, T, GLA_HEADS, GLA_DK).astype(f32)
        vh = v.reshape(B, T, GLA_HEADS, GLA_DV).astype(f32)
        gk = jnp.einsum('btr,rk->btk', g_lr, w_gate_up[l]) + b_gate_up[l]
        log_a = (jax.nn.log_sigmoid(gk.astype(f32)) / GATE_NORMALIZER).reshape(B, T, GLA_HEADS, GLA_DK)
        o = gla_chunked(qh, kh, vh, log_a)
        o = rms_norm(o, gla_norm_w[l]) * jax.nn.silu(g_out.reshape(B, T, GLA_HEADS, GLA_DV).astype(f32))
        o_gla = o.reshape(B, T, GLA_WIDTH).astype(x.dtype)

        su = jax.nn.gelu(su, approximate=False).reshape(B, T, SG_GROUPS, SG_CH)
        sv = jax.nn.gelu(sv, approximate=False).reshape(B, T, SG_GROUPS, SG_CH)
        o_sg = spatial_gate(su, sv, sg_ln_w[l], sg_ln_b[l], sg_w_s[l], sg_b_s[l]).reshape(B, T, SG_WIDTH)

        mix = jnp.concatenate([o_gla, o_sg], axis=-1)
        h = h + jnp.einsum('btc,cd->btd', mix, w_out[l])

        n2 = rms_norm(h, norm2_w[l])
        a = jnp.einsum('btd,df->btf', n2, w_ffn_gate[l])
        bu = jnp.einsum('btd,df->btf', n2, w_ffn_up[l])
        h = h + jnp.einsum('btf,fd->btd', jax.nn.silu(a) * bu, w_ffn_down[l])
    return rms_norm(h, final_norm_w)
```

```python
import functools

import jax
import jax.numpy as jnp
from jax import lax
from jax.experimental import pallas as pl
from jax.experimental.pallas import tpu as pltpu

F32 = jnp.float32
BF16 = jnp.bfloat16

D_MODEL = 1024
GLA_HEADS = 4
GLA_DK = 64
GLA_DV = 128
GLA_KEY = GLA_HEADS * GLA_DK
GLA_WIDTH = GLA_HEADS * GLA_DV
GATE_RANK = 16
GATE_NORMALIZER = 16.0
GLA_CHUNK = 64
SG_GROUPS = 4
SG_CH = 128
SG_WIDTH = SG_GROUPS * SG_CH
SG_CHUNK = 128
D_FF = 2816
NORM_EPS = 1e-5

LANES = 128
GLR_PAD = LANES

C_Q = 0
C_K = C_Q + GLA_KEY
C_V = C_K + GLA_KEY
C_GOUT = C_V + GLA_WIDTH
C_SU = C_GOUT + GLA_WIDTH
C_SV = C_SU + SG_WIDTH
C_GLR = C_SV + SG_WIDTH
IN_COLS_PAD = C_GLR + GLR_PAD

PAIR = 2 * GLA_CHUNK
CUM_GROUP = 256

MIX_TILE = 512
FFN_TILE = 512
FF_CHUNK = 256
VMEM_LIMIT = 56 * 1024 * 1024


def _rms(x, w):
    ms = jnp.mean(x * x, axis=-1, keepdims=True)
    return x * lax.rsqrt(ms + NORM_EPS) * w


def _gelu(x):
    return 0.5 * x * (1.0 + lax.erf(x * (0.5 ** 0.5)))


def _dot(a, b):
    return jnp.dot(a, b, preferred_element_type=F32)


def _dot_nt(a, b):
    return lax.dot_general(a, b, (((1,), (1,)), ((), ())), preferred_element_type=F32)


def _mixer_kernel(x_ref, n1w_ref, win_ref, wgu_ref, bgu_ref, gnw_ref, lnw_ref, lnb_ref,
                  ws_ref, bst_ref, wout_ref, o_ref, st_ref, ogla_ref, mix_ref):
    tt = x_ref.shape[1]

    @pl.when(pl.program_id(1) == 0)
    def _():
        st_ref[...] = jnp.zeros_like(st_ref)

    x = x_ref[0]
    n = _rms(x, n1w_ref[...]).astype(BF16)

    def proj(lo, hi):
        return _dot(n, win_ref[:, lo:hi])

    q = proj(C_Q, C_K)
    k = proj(C_K, C_V)
    v = proj(C_V, C_GOUT)
    glr = proj(C_GLR, IN_COLS_PAD)
    gk = _dot(glr.astype(BF16), wgu_ref[...]) + bgu_ref[...]
    log_a = (jnp.minimum(gk, 0.0) - jnp.log1p(jnp.exp(-jnp.abs(gk)))) * (1.0 / GATE_NORMALIZER)

    la_hi = log_a.astype(BF16)
    la_lo = (log_a - la_hi.astype(F32)).astype(BF16)
    r = lax.broadcasted_iota(jnp.int32, (CUM_GROUP, CUM_GROUP), 0)
    c = lax.broadcasted_iota(jnp.int32, (CUM_GROUP, CUM_GROUP), 1)
    tri = ((r // GLA_CHUNK == c // GLA_CHUNK) & (r >= c)).astype(BF16)
    b_parts = []
    for g in range(tt // CUM_GROUP):
        sl = slice(g * CUM_GROUP, (g + 1) * CUM_GROUP)
        b_parts.append(_dot(tri, la_hi[sl]) + _dot(tri, la_lo[sl]))

    lane_head = lax.broadcasted_iota(jnp.int32, (PAIR, GLA_KEY), 1) // GLA_DK
    pr = lax.broadcasted_iota(jnp.int32, (GLA_HEADS * PAIR, PAIR), 0) % PAIR
    pc = lax.broadcasted_iota(jnp.int32, (GLA_HEADS * PAIR, PAIR), 1)
    causal = (pr // GLA_CHUNK == pc // GLA_CHUNK) & (pr >= pc)
    sr = lax.broadcasted_iota(jnp.int32, (GLA_WIDTH, GLA_KEY), 0) // GLA_DV
    sc = lax.broadcasted_iota(jnp.int32, (GLA_WIDTH, GLA_KEY), 1) // GLA_DK
    head_diag = sr == sc
    krow = lax.broadcasted_iota(jnp.int32, (PAIR, GLA_KEY), 0)

    st = st_ref[...]
    for p in range(tt // PAIR):
        psl = slice(p * PAIR, (p + 1) * PAIR)
        bgrp = b_parts[(p * PAIR) // CUM_GROUP]
        off = (p * PAIR) % CUM_GROUP
        b = bgrp[off:off + PAIR]
        b_last0 = b[GLA_CHUNK - 1:GLA_CHUNK]
        b_last1 = b[PAIR - 1:PAIR]
        b_last = jnp.where(krow < GLA_CHUNK, b_last0, b_last1)
        qp = q[psl]
        kp = k[psl]
        vp = v[psl]
        q_dec = qp * (jnp.exp(b) * (GLA_DK ** -0.5))
        k_inv = (kp * jnp.exp(-b)).astype(BF16)
        k_end = kp * jnp.exp(b_last - b)
        q_dec_b = q_dec.astype(BF16)

        q_stack = jnp.concatenate(
            [jnp.where(lane_head == h, q_dec, 0.0) for h in range(GLA_HEADS)], axis=0).astype(BF16)
        scores = _dot_nt(q_stack, k_inv)
        probs = jnp.where(causal, scores, 0.0).astype(BF16)
        vp_b = vp.astype(BF16)
        o_intra = jnp.concatenate(
            [_dot(probs[h * PAIR:(h + 1) * PAIR], vp_b[:, h * GLA_DV:(h + 1) * GLA_DV])
             for h in range(GLA_HEADS)], axis=1)

        vt_b = vp.T.astype(BF16)
        for j in range(PAIR // GLA_CHUNK):
            rows = slice(j * GLA_CHUNK, (j + 1) * GLA_CHUNK)
            o_inter = _dot_nt(q_dec_b[rows], st.astype(BF16))
            ogla_ref[p * PAIR + j * GLA_CHUNK:p * PAIR + (j + 1) * GLA_CHUNK, :] = (
                o_intra[rows] + o_inter)
            in_chunk = (krow // GLA_CHUNK) == j
            k_end_j = jnp.where(in_chunk, k_end, 0.0).astype(BF16)
            u_t = _dot(vt_b, k_end_j)
            decay = jnp.exp(b_last0 if j == 0 else b_last1)
            st = st * decay + jnp.where(head_diag, u_t, 0.0)
    st_ref[...] = st

    g_out = proj(C_GOUT, C_SU)
    for h in range(GLA_HEADS):
        hs = slice(h * GLA_DV, (h + 1) * GLA_DV)
        o_h = _rms(ogla_ref[:, hs], gnw_ref[...])
        g_h = g_out[:, hs]
        mix_ref[:, hs] = (o_h * (g_h * jax.nn.sigmoid(g_h))).astype(BF16)

    su = _gelu(proj(C_SU, C_SV))
    sv = _gelu(proj(C_SV, C_GLR))
    wr = lax.broadcasted_iota(jnp.int32, (SG_CHUNK, SG_CHUNK), 0)
    wc = lax.broadcasted_iota(jnp.int32, (SG_CHUNK, SG_CHUNK), 1)
    for g in range(SG_GROUPS):
        gs = slice(g * SG_CH, (g + 1) * SG_CH)
        v_g = sv[:, gs]
        mu = jnp.mean(v_g, axis=-1, keepdims=True)
        d = v_g - mu
        var = jnp.mean(d * d, axis=-1, keepdims=True)
        y = (d * lax.rsqrt(var + NORM_EPS) * lnw_ref[g:g + 1, :] + lnb_ref[g:g + 1, :]).astype(BF16)
        w_g = jnp.where(wr >= wc, ws_ref[g], jnp.zeros((), BF16))
        bias = bst_ref[:, g:g + 1]
        for p in range(tt // SG_CHUNK):
            rs = slice(p * SG_CHUNK, (p + 1) * SG_CHUNK)
            mixed = _dot(w_g, y[rs]) + bias
            mix_ref[rs, GLA_WIDTH + g * SG_CH:GLA_WIDTH + (g + 1) * SG_CH] = (
                su[rs, gs] * mixed).astype(BF16)

    o_ref[0] = x + _dot(mix_ref[...], wout_ref[...])


def _ffn_kernel(h_ref, n2w_ref, wg_ref, wu_ref, wd_ref, fnw_ref, o_ref, hid_ref):
    h = h_ref[...]
    n2 = _rms(h, n2w_ref[...]).astype(BF16)
    for f in range(D_FF // FF_CHUNK):
        fs = slice(f * FF_CHUNK, (f + 1) * FF_CHUNK)
        a = _dot(n2, wg_ref[:, fs])
        u = _dot(n2, wu_ref[:, fs])
        hid_ref[:, fs] = (a * jax.nn.sigmoid(a) * u).astype(BF16)
    h2 = h + _dot(hid_ref[...], wd_ref[...])
    o_ref[...] = _rms(h2, fnw_ref[...])


def _const_spec(shape):
    nd = len(shape)
    return pl.BlockSpec(shape, lambda *_: (0,) * nd, pipeline_mode=pl.Buffered(1))


def _mixer(x, n1w, win, wgu, bgu, gnw, lnw, lnb, ws, bst, wout):
    bsz, seq, _ = x.shape
    assert seq % MIX_TILE == 0 and MIX_TILE % CUM_GROUP == 0
    consts = (n1w, win, wgu, bgu, gnw, lnw, lnb, ws, bst, wout)
    return pl.pallas_call(
        _mixer_kernel,
        grid=(bsz, seq // MIX_TILE),
        in_specs=[pl.BlockSpec((1, MIX_TILE, D_MODEL), lambda b, t: (b, t, 0))]
        + [_const_spec(a.shape) for a in consts],
        out_specs=pl.BlockSpec((1, MIX_TILE, D_MODEL), lambda b, t: (b, t, 0)),
        out_shape=jax.ShapeDtypeStruct(x.shape, F32),
        scratch_shapes=[
            pltpu.VMEM((GLA_WIDTH, GLA_KEY), F32),
            pltpu.VMEM((MIX_TILE, GLA_WIDTH), F32),
            pltpu.VMEM((MIX_TILE, D_MODEL), BF16),
        ],
        compiler_params=pltpu.CompilerParams(
            dimension_semantics=("parallel", "arbitrary"),
            vmem_limit_bytes=VMEM_LIMIT),
        name="mixer",
    )(x, *consts)


def _ffn(h, n2w, wg, wu, wd, fnw):
    m = h.shape[0]
    assert m % FFN_TILE == 0
    consts = (n2w, wg, wu, wd, fnw)
    return pl.pallas_call(
        _ffn_kernel,
        grid=(m // FFN_TILE,),
        in_specs=[pl.BlockSpec((FFN_TILE, D_MODEL), lambda i: (i, 0))]
        + [_const_spec(a.shape) for a in consts],
        out_specs=pl.BlockSpec((FFN_TILE, D_MODEL), lambda i: (i, 0)),
        out_shape=jax.ShapeDtypeStruct(h.shape, F32),
        scratch_shapes=[pltpu.VMEM((FFN_TILE, D_FF), BF16)],
        compiler_params=pltpu.CompilerParams(
            dimension_semantics=("parallel",),
            vmem_limit_bytes=VMEM_LIMIT),
        name="ffn",
    )(h, *consts)


def kernel(x, norm1_w, w_in, w_gate_up, b_gate_up, gla_norm_w, sg_ln_w, sg_ln_b, sg_w_s, sg_b_s,
           w_out, norm2_w, w_ffn_gate, w_ffn_up, w_ffn_down, final_norm_w):
    bsz, seq, _ = x.shape
    assert w_in.shape[0] == 1, "the final norm is fused into the (single) layer's ffn call"
    h = x
    for l in range(1):
        wi = w_in[l]
        o_v, o_glr = GLA_KEY + GLA_KEY + GLA_WIDTH, GLA_KEY + GLA_KEY + GLA_WIDTH + GATE_RANK
        win = jnp.concatenate(
            [wi[:, :o_v], wi[:, o_glr:], wi[:, o_v:o_glr],
             jnp.zeros((D_MODEL, GLR_PAD - GATE_RANK), wi.dtype)], axis=1).astype(BF16)
        wgu = jnp.concatenate(
            [w_gate_up[l], jnp.zeros((GLR_PAD - GATE_RANK, GLA_KEY), w_gate_up.dtype)],
            axis=0).astype(BF16)
        h = _mixer(
            h, norm1_w[l][None, :], win, wgu, b_gate_up[l][None, :], gla_norm_w[l][None, :],
            sg_ln_w[l], sg_ln_b[l], sg_w_s[l].astype(BF16), jnp.transpose(sg_b_s[l]),
            w_out[l].astype(BF16))
        h = _ffn(h.reshape(bsz * seq, D_MODEL), norm2_w[l][None, :], w_ffn_gate[l].astype(BF16),
                 w_ffn_up[l].astype(BF16), w_ffn_down[l].astype(BF16),
                 final_norm_w[None, :]).reshape(bsz, seq, D_MODEL)
    return h
```

```python
import jax
import jax.numpy as jnp
from jax import lax
from jax.experimental import pallas as pl
from jax.experimental.pallas import tpu as pltpu

F32 = jnp.float32
BF16 = jnp.bfloat16

D_MODEL = 1024
GLA_HEADS = 4
GLA_DK = 64
GLA_DV = 128
GLA_KEY = GLA_HEADS * GLA_DK
GLA_WIDTH = GLA_HEADS * GLA_DV
GATE_RANK = 16
GATE_NORMALIZER = 16.0
GLA_CHUNK = 64
SG_GROUPS = 4
SG_CH = 128
SG_WIDTH = SG_GROUPS * SG_CH
SG_CHUNK = 128
D_FF = 2816
NORM_EPS = 1e-5

LANES = 128
GLR_PAD = LANES

C_Q = 0
C_K = C_Q + GLA_KEY
C_V = C_K + GLA_KEY
C_GOUT = C_V + GLA_WIDTH
C_SU = C_GOUT + GLA_WIDTH
C_SV = C_SU + SG_WIDTH
C_GLR = C_SV + SG_WIDTH
IN_COLS_PAD = C_GLR + GLR_PAD

PAIR = 2 * GLA_CHUNK
CUM_GROUP = 256
PROJ_COLS = 256

MIX_TILE = 512
FFN_TILE = 512
FF_CHUNK = 256
VMEM_LIMIT = 56 * 1024 * 1024


def _rms(x, w):
    ms = jnp.mean(x * x, axis=-1, keepdims=True)
    return x * lax.rsqrt(ms + NORM_EPS) * w


def _gelu(x):
    return 0.5 * x * (1.0 + lax.erf(x * (0.5 ** 0.5)))


def _dot(a, b):
    return jnp.dot(a, b, preferred_element_type=F32)


def _dot_nt(a, b):
    return lax.dot_general(a, b, (((1,), (1,)), ((), ())), preferred_element_type=F32)


def _mixer_kernel(x_ref, n1w_ref, win_ref, wgu_ref, bgu_ref, gnw_ref, lnw_ref, lnb_ref,
                  ws_ref, bst_ref, wout_ref, o_ref, st_ref, ogla_ref, mix_ref):
    tt = x_ref.shape[1]
    n_pairs = tt // PAIR

    @pl.when(pl.program_id(1) == 0)
    def _():
        st_ref[...] = jnp.zeros_like(st_ref)

    x = x_ref[0]
    n = _rms(x, n1w_ref[...]).astype(BF16)

    def proj(lo, hi):
        return _dot(n, win_ref[:, lo:hi])

    glr = proj(C_GLR, IN_COLS_PAD)
    gk = _dot(glr.astype(BF16), wgu_ref[...]) + bgu_ref[...]
    q = proj(C_Q, C_K)
    k = proj(C_K, C_V)
    v = proj(C_V, C_GOUT)
    log_a = (jnp.minimum(gk, 0.0) - jnp.log1p(jnp.exp(-jnp.abs(gk)))) * (1.0 / GATE_NORMALIZER)

    la_hi = log_a.astype(BF16)
    la_lo = (log_a - la_hi.astype(F32)).astype(BF16)
    r = lax.broadcasted_iota(jnp.int32, (CUM_GROUP, CUM_GROUP), 0)
    c = lax.broadcasted_iota(jnp.int32, (CUM_GROUP, CUM_GROUP), 1)
    tri = ((r // PAIR == c // PAIR) & (r >= c)).astype(BF16)
    b_parts = []
    for g in range(tt // CUM_GROUP):
        sl = slice(g * CUM_GROUP, (g + 1) * CUM_GROUP)
        b_parts.append(_dot(tri, la_hi[sl]) + _dot(tri, la_lo[sl]))

    lane_head = lax.broadcasted_iota(jnp.int32, (PAIR, GLA_KEY), 1) // GLA_DK
    pr = lax.broadcasted_iota(jnp.int32, (GLA_HEADS * PAIR, PAIR), 0) % PAIR
    pc = lax.broadcasted_iota(jnp.int32, (GLA_HEADS * PAIR, PAIR), 1)
    causal = pr >= pc
    sr = lax.broadcasted_iota(jnp.int32, (GLA_WIDTH, GLA_KEY), 0) // GLA_DV
    sc = lax.broadcasted_iota(jnp.int32, (GLA_WIDTH, GLA_KEY), 1) // GLA_DK
    head_diag = sr == sc

    def gla_pair(p, st):
        psl = slice(p * PAIR, (p + 1) * PAIR)
        off = (p * PAIR) % CUM_GROUP
        b = b_parts[(p * PAIR) // CUM_GROUP][off:off + PAIR]
        b_mid = b[GLA_CHUNK - 1:GLA_CHUNK]
        b_end = b[PAIR - 1:PAIR]
        qs = q[psl] * (GLA_DK ** -0.5)
        kp = k[psl]
        vp = v[psl]
        q_rel = qs * jnp.exp(b - b_mid)
        k_rel = (kp * jnp.exp(b_mid - b)).astype(BF16)
        q_in = (qs * jnp.exp(b)).astype(BF16)
        k_out = (kp * jnp.exp(b_end - b)).astype(BF16)

        q_stack = jnp.concatenate(
            [jnp.where(lane_head == h, q_rel, 0.0) for h in range(GLA_HEADS)], axis=0).astype(BF16)
        probs = jnp.where(causal, _dot_nt(q_stack, k_rel), 0.0).astype(BF16)
        vp_b = vp.astype(BF16)
        o_intra = jnp.concatenate(
            [_dot(probs[h * PAIR:(h + 1) * PAIR], vp_b[:, h * GLA_DV:(h + 1) * GLA_DV])
             for h in range(GLA_HEADS)], axis=1)
        o_inter = _dot_nt(q_in, st.astype(BF16))
        ogla_ref[psl, :] = o_intra + o_inter
        u_t = _dot(vp.T.astype(BF16), k_out)
        return st * jnp.exp(b_end) + jnp.where(head_diag, u_t, 0.0)

    fill = [(name, lo + j * PROJ_COLS)
            for name, lo in (("sv", C_SV), ("su", C_SU), ("go", C_GOUT))
            for j in range(SG_WIDTH // PROJ_COLS)]
    wide = {"sv": [], "su": [], "go": []}
    lead = len(fill) - n_pairs
    for name, lo in fill[:lead]:
        wide[name].append(proj(lo, lo + PROJ_COLS))
    st = st_ref[...]
    for p in range(n_pairs):
        st = gla_pair(p, st)
        name, lo = fill[lead + p]
        wide[name].append(proj(lo, lo + PROJ_COLS))
    st_ref[...] = st

    def cols(name, g):
        per = PROJ_COLS // LANES
        return wide[name][g // per][:, (g % per) * LANES:(g % per + 1) * LANES]

    for h in range(GLA_HEADS):
        hs = slice(h * GLA_DV, (h + 1) * GLA_DV)
        o_h = _rms(ogla_ref[:, hs], gnw_ref[...])
        g_h = cols("go", h)
        mix_ref[:, hs] = (o_h * (g_h * jax.nn.sigmoid(g_h))).astype(BF16)
    acc = x + _dot(mix_ref[:, :GLA_WIDTH], wout_ref[:GLA_WIDTH, :])

    wr = lax.broadcasted_iota(jnp.int32, (SG_CHUNK, SG_CHUNK), 0)
    wc = lax.broadcasted_iota(jnp.int32, (SG_CHUNK, SG_CHUNK), 1)
    for g in range(SG_GROUPS):
        v_g = _gelu(cols("sv", g))
        u_g = _gelu(cols("su", g))
        mu = jnp.mean(v_g, axis=-1, keepdims=True)
        d = v_g - mu
        var = jnp.mean(d * d, axis=-1, keepdims=True)
        y = (d * lax.rsqrt(var + NORM_EPS) * lnw_ref[g:g + 1, :] + lnb_ref[g:g + 1, :]).astype(BF16)
        w_g = jnp.where(wr >= wc, ws_ref[g], jnp.zeros((), BF16))
        bias = bst_ref[:, g:g + 1]
        for p in range(tt // SG_CHUNK):
            rs = slice(p * SG_CHUNK, (p + 1) * SG_CHUNK)
            mixed = _dot(w_g, y[rs]) + bias
            mix_ref[rs, GLA_WIDTH + g * SG_CH:GLA_WIDTH + (g + 1) * SG_CH] = (
                u_g[rs] * mixed).astype(BF16)

    o_ref[0] = acc + _dot(mix_ref[:, GLA_WIDTH:], wout_ref[GLA_WIDTH:, :])


def _ffn_kernel(h_ref, n2w_ref, wg_ref, wu_ref, wd_ref, fnw_ref, o_ref, hid_ref):
    h = h_ref[...]
    n2 = _rms(h, n2w_ref[...]).astype(BF16)
    for f in range(D_FF // FF_CHUNK):
        fs = slice(f * FF_CHUNK, (f + 1) * FF_CHUNK)
        a = _dot(n2, wg_ref[:, fs])
        u = _dot(n2, wu_ref[:, fs])
        hid_ref[:, fs] = (a * jax.nn.sigmoid(a) * u).astype(BF16)
    h2 = h + _dot(hid_ref[...], wd_ref[...])
    o_ref[...] = _rms(h2, fnw_ref[...])


def _const_spec(shape):
    nd = len(shape)
    return pl.BlockSpec(shape, lambda *_: (0,) * nd, pipeline_mode=pl.Buffered(1))


def _mixer(x, n1w, win, wgu, bgu, gnw, lnw, lnb, ws, bst, wout):
    bsz, seq, _ = x.shape
    assert seq % MIX_TILE == 0 and MIX_TILE % CUM_GROUP == 0
    consts = (n1w, win, wgu, bgu, gnw, lnw, lnb, ws, bst, wout)
    return pl.pallas_call(
        _mixer_kernel,
        grid=(bsz, seq // MIX_TILE),
        in_specs=[pl.BlockSpec((1, MIX_TILE, D_MODEL), lambda b, t: (b, t, 0))]
        + [_const_spec(a.shape) for a in consts],
        out_specs=pl.BlockSpec((1, MIX_TILE, D_MODEL), lambda b, t: (b, t, 0)),
        out_shape=jax.ShapeDtypeStruct(x.shape, F32),
        scratch_shapes=[
            pltpu.VMEM((GLA_WIDTH, GLA_KEY), F32),
            pltpu.VMEM((MIX_TILE, GLA_WIDTH), F32),
            pltpu.VMEM((MIX_TILE, D_MODEL), BF16),
        ],
        compiler_params=pltpu.CompilerParams(
            dimension_semantics=("parallel", "arbitrary"),
            vmem_limit_bytes=VMEM_LIMIT),
        name="mixer",
    )(x, *consts)


def _ffn(h, n2w, wg, wu, wd, fnw):
    m = h.shape[0]
    assert m % FFN_TILE == 0
    consts = (n2w, wg, wu, wd, fnw)
    return pl.pallas_call(
        _ffn_kernel,
        grid=(m // FFN_TILE,),
        in_specs=[pl.BlockSpec((FFN_TILE, D_MODEL), lambda i: (i, 0))]
        + [_const_spec(a.shape) for a in consts],
        out_specs=pl.BlockSpec((FFN_TILE, D_MODEL), lambda i: (i, 0)),
        out_shape=jax.ShapeDtypeStruct(h.shape, F32),
        scratch_shapes=[pltpu.VMEM((FFN_TILE, D_FF), BF16)],
        compiler_params=pltpu.CompilerParams(
            dimension_semantics=("parallel",),
            vmem_limit_bytes=VMEM_LIMIT),
        name="ffn",
    )(h, *consts)


def kernel(x, norm1_w, w_in, w_gate_up, b_gate_up, gla_norm_w, sg_ln_w, sg_ln_b, sg_w_s, sg_b_s,
           w_out, norm2_w, w_ffn_gate, w_ffn_up, w_ffn_down, final_norm_w):
    bsz, seq, _ = x.shape
    assert w_in.shape[0] == 1, "the final norm is fused into the (single) layer's ffn call"
    wi = w_in[0]
    o_v = GLA_KEY + GLA_KEY + GLA_WIDTH
    o_glr = o_v + GATE_RANK
    win = jnp.concatenate(
        [wi[:, :o_v], wi[:, o_glr:], wi[:, o_v:o_glr],
         jnp.zeros((D_MODEL, GLR_PAD - GATE_RANK), wi.dtype)], axis=1).astype(BF16)
    wgu = jnp.concatenate(
        [w_gate_up[0], jnp.zeros((GLR_PAD - GATE_RANK, GLA_KEY), w_gate_up.dtype)],
        axis=0).astype(BF16)
    h = _mixer(
        x, norm1_w[0][None, :], win, wgu, b_gate_up[0][None, :], gla_norm_w[0][None, :],
        sg_ln_w[0], sg_ln_b[0], sg_w_s[0].astype(BF16), jnp.transpose(sg_b_s[0]),
        w_out[0].astype(BF16))
    out = _ffn(h.reshape(bsz * seq, D_MODEL), norm2_w[0][None, :], w_ffn_gate[0].astype(BF16),
               w_ffn_up[0].astype(BF16), w_ffn_down[0].astype(BF16), final_norm_w[None, :])
    return out.reshape(bsz, seq, D_MODEL)
```

```python
import jax
import jax.numpy as jnp
from jax import lax
from jax.experimental import pallas as pl
from jax.experimental.pallas import tpu as pltpu

F32 = jnp.float32
BF16 = jnp.bfloat16

D_MODEL = 1024
GLA_HEADS = 4
GLA_DK = 64
GLA_DV = 128
GLA_KEY = GLA_HEADS * GLA_DK
GLA_WIDTH = GLA_HEADS * GLA_DV
GATE_RANK = 16
GATE_NORMALIZER = 16.0
GLA_CHUNK = 64
SG_GROUPS = 4
SG_CH = 128
SG_WIDTH = SG_GROUPS * SG_CH
SG_CHUNK = 128
D_FF = 2816
NORM_EPS = 1e-5

LANES = 128
GLR_PAD = LANES

C_Q = 0
C_K = C_Q + GLA_KEY
C_V = C_K + GLA_KEY
C_GOUT = C_V + GLA_WIDTH
C_SU = C_GOUT + GLA_WIDTH
C_SV = C_SU + SG_WIDTH
C_GLR = C_SV + SG_WIDTH
IN_COLS_PAD = C_GLR + GLR_PAD

PAIR = 2 * GLA_CHUNK
CUM_GROUP = 256
PROJ_COLS = 256

MIX_TILE = 512
FFN_TILE = 512
FF_CHUNK = 256
VMEM_LIMIT = 56 * 1024 * 1024


def _rms(x, w):
    ms = jnp.mean(x * x, axis=-1, keepdims=True)
    return x * lax.rsqrt(ms + NORM_EPS) * w


def _gelu(x):
    return 0.5 * x * (1.0 + lax.erf(x * (0.5 ** 0.5)))


def _dot(a, b):
    return jnp.dot(a, b, preferred_element_type=F32)


def _dot_nt(a, b):
    return lax.dot_general(a, b, (((1,), (1,)), ((), ())), preferred_element_type=F32)


def _mixer_kernel(x_ref, n1w_ref, win_ref, wgu_ref, bgu_ref, gnw_ref, lnw_ref, lnb_ref,
                  ws_ref, bst_ref, wout_ref, o_ref, st_ref, ogla_ref, mix_ref):
    tt = x_ref.shape[1]
    n_pairs = tt // PAIR
    per = PROJ_COLS // LANES

    @pl.when(pl.program_id(1) == 0)
    def _():
        st_ref[...] = jnp.zeros_like(st_ref)

    x = x_ref[0]
    n = _rms(x, n1w_ref[...]).astype(BF16)

    def proj(lo, width):
        return _dot(n, win_ref[:, lo:lo + width])

    def proj_wide(lo, j):
        return proj(lo + j * PROJ_COLS, PROJ_COLS)

    def cols(pieces, g):
        return pieces[g // per][:, (g % per) * LANES:(g % per + 1) * LANES]

    glr = proj(C_GLR, GLR_PAD)
    sv = [proj_wide(C_SV, 0)]
    gk = _dot(glr.astype(BF16), wgu_ref[...]) + bgu_ref[...]
    sv.append(proj_wide(C_SV, 1))
    log_a = (jnp.minimum(gk, 0.0) - jnp.log1p(jnp.exp(-jnp.abs(gk)))) * (1.0 / GATE_NORMALIZER)
    k = proj(C_K, GLA_KEY)
    q = proj(C_Q, GLA_KEY)

    la_hi = log_a.astype(BF16)
    la_lo = (log_a - la_hi.astype(F32)).astype(BF16)
    r = lax.broadcasted_iota(jnp.int32, (CUM_GROUP, CUM_GROUP), 0)
    c = lax.broadcasted_iota(jnp.int32, (CUM_GROUP, CUM_GROUP), 1)
    tri = ((r // PAIR == c // PAIR) & (r >= c)).astype(BF16)
    b_parts = []
    for g in range(tt // CUM_GROUP):
        sl = slice(g * CUM_GROUP, (g + 1) * CUM_GROUP)
        b_parts.append(_dot(tri, la_hi[sl]) + _dot(tri, la_lo[sl]))
    v = proj(C_V, GLA_WIDTH)

    y = []
    for g in range(SG_GROUPS):
        v_g = _gelu(cols(sv, g))
        mu = jnp.mean(v_g, axis=-1, keepdims=True)
        d = v_g - mu
        var = jnp.mean(d * d, axis=-1, keepdims=True)
        y.append((d * lax.rsqrt(var + NORM_EPS) * lnw_ref[g:g + 1, :]
                  + lnb_ref[g:g + 1, :]).astype(BF16))

    lane_head = lax.broadcasted_iota(jnp.int32, (PAIR, GLA_KEY), 1) // GLA_DK
    pr = lax.broadcasted_iota(jnp.int32, (GLA_HEADS * PAIR, PAIR), 0) % PAIR
    pc = lax.broadcasted_iota(jnp.int32, (GLA_HEADS * PAIR, PAIR), 1)
    causal = pr >= pc
    sr = lax.broadcasted_iota(jnp.int32, (GLA_WIDTH, GLA_KEY), 0) // GLA_DV
    sc = lax.broadcasted_iota(jnp.int32, (GLA_WIDTH, GLA_KEY), 1) // GLA_DK
    head_diag = sr == sc

    def gla_scores(p):
        psl = slice(p * PAIR, (p + 1) * PAIR)
        off = (p * PAIR) % CUM_GROUP
        b = b_parts[(p * PAIR) // CUM_GROUP][off:off + PAIR]
        b_mid = b[GLA_CHUNK - 1:GLA_CHUNK]
        b_end = b[PAIR - 1:PAIR]
        qs = q[psl] * (GLA_DK ** -0.5)
        kp = k[psl]
        q_rel = qs * jnp.exp(b - b_mid)
        k_rel = (kp * jnp.exp(b_mid - b)).astype(BF16)
        q_in = (qs * jnp.exp(b)).astype(BF16)
        k_out = (kp * jnp.exp(b_end - b)).astype(BF16)
        q_stack = jnp.concatenate(
            [jnp.where(lane_head == h, q_rel, 0.0) for h in range(GLA_HEADS)], axis=0).astype(BF16)
        probs = jnp.where(causal, _dot_nt(q_stack, k_rel), 0.0).astype(BF16)
        return probs, q_in, k_out, jnp.exp(b_end)

    def gla_apply(p, st, probs, q_in, k_out, decay):
        psl = slice(p * PAIR, (p + 1) * PAIR)
        vp = v[psl]
        vp_b = vp.astype(BF16)
        o_intra = jnp.concatenate(
            [_dot(probs[h * PAIR:(h + 1) * PAIR], vp_b[:, h * GLA_DV:(h + 1) * GLA_DV])
             for h in range(GLA_HEADS)], axis=1)
        o_inter = _dot_nt(q_in, st.astype(BF16))
        ogla_ref[psl, :] = o_intra + o_inter
        u_t = _dot(vp.T.astype(BF16), k_out)
        return st * decay + jnp.where(head_diag, u_t, 0.0)

    wr = lax.broadcasted_iota(jnp.int32, (SG_CHUNK, SG_CHUNK), 0)
    wc = lax.broadcasted_iota(jnp.int32, (SG_CHUNK, SG_CHUNK), 1)

    def sg_mix(su):
        for g in range(SG_GROUPS):
            u_g = _gelu(cols(su, g))
            w_g = jnp.where(wr >= wc, ws_ref[g], jnp.zeros((), BF16))
            bias = bst_ref[:, g:g + 1]
            for p in range(tt // SG_CHUNK):
                rs = slice(p * SG_CHUNK, (p + 1) * SG_CHUNK)
                mixed = _dot(w_g, y[g][rs]) + bias
                mix_ref[rs, GLA_WIDTH + g * SG_CH:GLA_WIDTH + (g + 1) * SG_CH] = (
                    u_g[rs] * mixed).astype(BF16)

    su = [proj_wide(C_SU, j) for j in range(SG_WIDTH // PROJ_COLS)]
    go = []
    acc = []
    fills = [lambda: sg_mix(su),
             lambda: go.append(proj_wide(C_GOUT, 0)),
             lambda: go.append(proj_wide(C_GOUT, 1)),
             lambda: acc.append(x + _dot(mix_ref[:, GLA_WIDTH:], wout_ref[GLA_WIDTH:, :]))]
    assert len(fills) == n_pairs
    st = st_ref[...]
    staged = gla_scores(0)
    for p in range(n_pairs):
        fills[p]()
        nxt = gla_scores(p + 1) if p + 1 < n_pairs else None
        st = gla_apply(p, st, *staged)
        staged = nxt
    st_ref[...] = st

    acc = acc[0]
    for h in range(GLA_HEADS):
        hs = slice(h * GLA_DV, (h + 1) * GLA_DV)
        o_h = _rms(ogla_ref[:, hs], gnw_ref[...])
        g_h = cols(go, h)
        mix_ref[:, hs] = (o_h * (g_h * jax.nn.sigmoid(g_h))).astype(BF16)
    o_ref[0] = acc + _dot(mix_ref[:, :GLA_WIDTH], wout_ref[:GLA_WIDTH, :])


def _ffn_kernel(h_ref, n2w_ref, wg_ref, wu_ref, wd_ref, fnw_ref, o_ref, hid_ref):
    h = h_ref[...]
    n2 = _rms(h, n2w_ref[...]).astype(BF16)
    for f in range(D_FF // FF_CHUNK):
        fs = slice(f * FF_CHUNK, (f + 1) * FF_CHUNK)
        a = _dot(n2, wg_ref[:, fs])
        u = _dot(n2, wu_ref[:, fs])
        hid_ref[:, fs] = (a * jax.nn.sigmoid(a) * u).astype(BF16)
    h2 = h + _dot(hid_ref[...], wd_ref[...])
    o_ref[...] = _rms(h2, fnw_ref[...])


def _const_spec(shape):
    nd = len(shape)
    return pl.BlockSpec(shape, lambda *_: (0,) * nd, pipeline_mode=pl.Buffered(1))


def _mixer(x, n1w, win, wgu, bgu, gnw, lnw, lnb, ws, bst, wout):
    bsz, seq, _ = x.shape
    assert seq % MIX_TILE == 0 and MIX_TILE % CUM_GROUP == 0
    consts = (n1w, win, wgu, bgu, gnw, lnw, lnb, ws, bst, wout)
    return pl.pallas_call(
        _mixer_kernel,
        grid=(bsz, seq // MIX_TILE),
        in_specs=[pl.BlockSpec((1, MIX_TILE, D_MODEL), lambda b, t: (b, t, 0))]
        + [_const_spec(a.shape) for a in consts],
        out_specs=pl.BlockSpec((1, MIX_TILE, D_MODEL), lambda b, t: (b, t, 0)),
        out_shape=jax.ShapeDtypeStruct(x.shape, F32),
        scratch_shapes=[
            pltpu.VMEM((GLA_WIDTH, GLA_KEY), F32),
            pltpu.VMEM((MIX_TILE, GLA_WIDTH), F32),
            pltpu.VMEM((MIX_TILE, D_MODEL), BF16),
        ],
        compiler_params=pltpu.CompilerParams(
            dimension_semantics=("parallel", "arbitrary"),
            vmem_limit_bytes=VMEM_LIMIT),
        name="mixer",
    )(x, *consts)


def _ffn(h, n2w, wg, wu, wd, fnw):
    m = h.shape[0]
    assert m % FFN_TILE == 0
    consts = (n2w, wg, wu, wd, fnw)
    return pl.pallas_call(
        _ffn_kernel,
        grid=(m // FFN_TILE,),
        in_specs=[pl.BlockSpec((FFN_TILE, D_MODEL), lambda i: (i, 0))]
        + [_const_spec(a.shape) for a in consts],
        out_specs=pl.BlockSpec((FFN_TILE, D_MODEL), lambda i: (i, 0)),
        out_shape=jax.ShapeDtypeStruct(h.shape, F32),
        scratch_shapes=[pltpu.VMEM((FFN_TILE, D_FF), BF16)],
        compiler_params=pltpu.CompilerParams(
            dimension_semantics=("parallel",),
            vmem_limit_bytes=VMEM_LIMIT),
        name="ffn",
    )(h, *consts)


def kernel(x, norm1_w, w_in, w_gate_up, b_gate_up, gla_norm_w, sg_ln_w, sg_ln_b, sg_w_s, sg_b_s,
           w_out, norm2_w, w_ffn_gate, w_ffn_up, w_ffn_down, final_norm_w):
    bsz, seq, _ = x.shape
    assert w_in.shape[0] == 1, "the final norm is fused into the (single) layer's ffn call"
    wi = w_in[0]
    o_v = GLA_KEY + GLA_KEY + GLA_WIDTH
    o_glr = o_v + GATE_RANK
    win = jnp.concatenate(
        [wi[:, :o_v], wi[:, o_glr:], wi[:, o_v:o_glr],
         jnp.zeros((D_MODEL, GLR_PAD - GATE_RANK), wi.dtype)], axis=1).astype(BF16)
    wgu = jnp.concatenate(
        [w_gate_up[0], jnp.zeros((GLR_PAD - GATE_RANK, GLA_KEY), w_gate_up.dtype)],
        axis=0).astype(BF16)
    h = _mixer(
        x, norm1_w[0][None, :], win, wgu, b_gate_up[0][None, :], gla_norm_w[0][None, :],
        sg_ln_w[0], sg_ln_b[0], sg_w_s[0].astype(BF16), jnp.transpose(sg_b_s[0]),
        w_out[0].astype(BF16))
    out = _ffn(h.reshape(bsz * seq, D_MODEL), norm2_w[0][None, :], w_ffn_gate[0].astype(BF16),
               w_ffn_up[0].astype(BF16), w_ffn_down[0].astype(BF16), final_norm_w[None, :])
    return out.reshape(bsz, seq, D_MODEL)
```

```python
import jax
import jax.numpy as jnp
from jax import lax
from jax.experimental import pallas as pl
from jax.experimental.pallas import tpu as pltpu

F32 = jnp.float32
BF16 = jnp.bfloat16

D_MODEL = 1024
GLA_HEADS = 4
GLA_DK = 64
GLA_DV = 128
GLA_KEY = GLA_HEADS * GLA_DK
GLA_WIDTH = GLA_HEADS * GLA_DV
GATE_RANK = 16
GATE_NORMALIZER = 16.0
GLA_CHUNK = 64
SG_GROUPS = 4
SG_CH = 128
SG_WIDTH = SG_GROUPS * SG_CH
SG_CHUNK = 128
D_FF = 2816
NORM_EPS = 1e-5

LANES = 128
GLR_PAD = LANES

C_Q = 0
C_K = C_Q + GLA_KEY
C_V = C_K + GLA_KEY
C_GOUT = C_V + GLA_WIDTH
C_SU = C_GOUT + GLA_WIDTH
C_SV = C_SU + SG_WIDTH
C_GLR = C_SV + SG_WIDTH
IN_COLS_PAD = C_GLR + GLR_PAD

PAIR = 2 * GLA_CHUNK
CUM_GROUP = 256
PROJ_COLS = 256

MIX_TILE = 1024
MIX_SUB = 512
FFN_TILE = 1024
FFN_SUB = 512
FF_CHUNK = 256
VMEM_LIMIT = 56 * 1024 * 1024


def _rms(x, w):
    ms = jnp.mean(x * x, axis=-1, keepdims=True)
    return x * lax.rsqrt(ms + NORM_EPS) * w


def _gelu(x):
    return 0.5 * x * (1.0 + lax.erf(x * (0.5 ** 0.5)))


def _dot(a, b):
    return jnp.dot(a, b, preferred_element_type=F32)


def _dot_nt(a, b):
    return lax.dot_general(a, b, (((1,), (1,)), ((), ())), preferred_element_type=F32)


def _mixer_kernel(x_ref, n1w_ref, win_ref, wgu_ref, bgu_ref, gnw_ref, lnw_ref, lnb_ref,
                  ws_ref, bst_ref, wout_ref, o_ref, st_ref, ogla_ref, mix_ref):
    @pl.when(pl.program_id(1) == 0)
    def _():
        st_ref[...] = jnp.zeros_like(st_ref)

    st = st_ref[...]
    for s in range(x_ref.shape[1] // MIX_SUB):
        st = _mixer_sub_tile(s * MIX_SUB, st, x_ref, n1w_ref, win_ref, wgu_ref, bgu_ref, gnw_ref,
                             lnw_ref, lnb_ref, ws_ref, bst_ref, wout_ref, o_ref, ogla_ref, mix_ref)
    st_ref[...] = st


def _mixer_sub_tile(base, st, x_ref, n1w_ref, win_ref, wgu_ref, bgu_ref, gnw_ref, lnw_ref, lnb_ref,
                    ws_ref, bst_ref, wout_ref, o_ref, ogla_ref, mix_ref):
    tt = MIX_SUB
    n_pairs = tt // PAIR
    per = PROJ_COLS // LANES
    rows = slice(base, base + tt)

    x = x_ref[0, rows, :]
    n = _rms(x, n1w_ref[...]).astype(BF16)

    def proj(lo, width):
        return _dot(n, win_ref[:, lo:lo + width])

    def proj_wide(lo, j):
        return proj(lo + j * PROJ_COLS, PROJ_COLS)

    def cols(pieces, g):
        return pieces[g // per][:, (g % per) * LANES:(g % per + 1) * LANES]

    glr = proj(C_GLR, GLR_PAD)
    sv = [proj_wide(C_SV, 0)]
    gk = _dot(glr.astype(BF16), wgu_ref[...]) + bgu_ref[...]
    sv.append(proj_wide(C_SV, 1))
    log_a = (jnp.minimum(gk, 0.0) - jnp.log1p(jnp.exp(-jnp.abs(gk)))) * (1.0 / GATE_NORMALIZER)
    k = proj(C_K, GLA_KEY)
    q = proj(C_Q, GLA_KEY)

    la_hi = log_a.astype(BF16)
    la_lo = (log_a - la_hi.astype(F32)).astype(BF16)
    r = lax.broadcasted_iota(jnp.int32, (CUM_GROUP, CUM_GROUP), 0)
    c = lax.broadcasted_iota(jnp.int32, (CUM_GROUP, CUM_GROUP), 1)
    tri = ((r // PAIR == c // PAIR) & (r >= c)).astype(BF16)
    b_parts = []
    for g in range(tt // CUM_GROUP):
        sl = slice(g * CUM_GROUP, (g + 1) * CUM_GROUP)
        b_parts.append(_dot(tri, la_hi[sl]) + _dot(tri, la_lo[sl]))
    v = proj(C_V, GLA_WIDTH)

    y = []
    for g in range(SG_GROUPS):
        v_g = _gelu(cols(sv, g))
        mu = jnp.mean(v_g, axis=-1, keepdims=True)
        d = v_g - mu
        var = jnp.mean(d * d, axis=-1, keepdims=True)
        y.append((d * lax.rsqrt(var + NORM_EPS) * lnw_ref[g:g + 1, :]
                  + lnb_ref[g:g + 1, :]).astype(BF16))

    lane_head = lax.broadcasted_iota(jnp.int32, (PAIR, GLA_KEY), 1) // GLA_DK
    pr = lax.broadcasted_iota(jnp.int32, (GLA_HEADS * PAIR, PAIR), 0) % PAIR
    pc = lax.broadcasted_iota(jnp.int32, (GLA_HEADS * PAIR, PAIR), 1)
    causal = pr >= pc
    sr = lax.broadcasted_iota(jnp.int32, (GLA_WIDTH, GLA_KEY), 0) // GLA_DV
    sc = lax.broadcasted_iota(jnp.int32, (GLA_WIDTH, GLA_KEY), 1) // GLA_DK
    head_diag = sr == sc

    def gla_scores(p):
        psl = slice(p * PAIR, (p + 1) * PAIR)
        off = (p * PAIR) % CUM_GROUP
        b = b_parts[(p * PAIR) // CUM_GROUP][off:off + PAIR]
        b_mid = b[GLA_CHUNK - 1:GLA_CHUNK]
        b_end = b[PAIR - 1:PAIR]
        qs = q[psl] * (GLA_DK ** -0.5)
        kp = k[psl]
        q_rel = qs * jnp.exp(b - b_mid)
        k_rel = (kp * jnp.exp(b_mid - b)).astype(BF16)
        q_in = (qs * jnp.exp(b)).astype(BF16)
        k_out = (kp * jnp.exp(b_end - b)).astype(BF16)
        q_stack = jnp.concatenate(
            [jnp.where(lane_head == h, q_rel, 0.0) for h in range(GLA_HEADS)], axis=0).astype(BF16)
        probs = jnp.where(causal, _dot_nt(q_stack, k_rel), 0.0).astype(BF16)
        return probs, q_in, k_out, jnp.exp(b_end)

    def gla_apply(p, st, probs, q_in, k_out, decay):
        psl = slice(p * PAIR, (p + 1) * PAIR)
        vp = v[psl]
        vp_b = vp.astype(BF16)
        o_intra = jnp.concatenate(
            [_dot(probs[h * PAIR:(h + 1) * PAIR], vp_b[:, h * GLA_DV:(h + 1) * GLA_DV])
             for h in range(GLA_HEADS)], axis=1)
        o_inter = _dot_nt(q_in, st.astype(BF16))
        ogla_ref[base + p * PAIR:base + (p + 1) * PAIR, :] = o_intra + o_inter
        u_t = _dot(vp.T.astype(BF16), k_out)
        return st * decay + jnp.where(head_diag, u_t, 0.0)

    wr = lax.broadcasted_iota(jnp.int32, (SG_CHUNK, SG_CHUNK), 0)
    wc = lax.broadcasted_iota(jnp.int32, (SG_CHUNK, SG_CHUNK), 1)

    def sg_mix(su):
        for g in range(SG_GROUPS):
            u_g = _gelu(cols(su, g))
            w_g = jnp.where(wr >= wc, ws_ref[g], jnp.zeros((), BF16))
            bias = bst_ref[:, g:g + 1]
            for p in range(tt // SG_CHUNK):
                rs = slice(p * SG_CHUNK, (p + 1) * SG_CHUNK)
                mixed = _dot(w_g, y[g][rs]) + bias
                mix_ref[base + p * SG_CHUNK:base + (p + 1) * SG_CHUNK,
                        GLA_WIDTH + g * SG_CH:GLA_WIDTH + (g + 1) * SG_CH] = (
                            u_g[rs] * mixed).astype(BF16)

    su = [proj_wide(C_SU, j) for j in range(SG_WIDTH // PROJ_COLS)]
    go = []
    acc = []
    fills = [lambda: sg_mix(su),
             lambda: go.append(proj_wide(C_GOUT, 0)),
             lambda: go.append(proj_wide(C_GOUT, 1)),
             lambda: acc.append(x + _dot(mix_ref[rows, GLA_WIDTH:], wout_ref[GLA_WIDTH:, :]))]
    assert len(fills) == n_pairs
    staged = gla_scores(0)
    for p in range(n_pairs):
        fills[p]()
        nxt = gla_scores(p + 1) if p + 1 < n_pairs else None
        st = gla_apply(p, st, *staged)
        staged = nxt

    acc = acc[0]
    for h in range(GLA_HEADS):
        hs = slice(h * GLA_DV, (h + 1) * GLA_DV)
        o_h = _rms(ogla_ref[rows, hs], gnw_ref[...])
        g_h = cols(go, h)
        mix_ref[rows, hs] = (o_h * (g_h * jax.nn.sigmoid(g_h))).astype(BF16)
    o_ref[0, rows, :] = acc + _dot(mix_ref[rows, :GLA_WIDTH], wout_ref[:GLA_WIDTH, :])
    return st


def _ffn_kernel(h_ref, n2w_ref, wg_ref, wu_ref, wd_ref, fnw_ref, o_ref, hid_ref):
    for s in range(h_ref.shape[0] // FFN_SUB):
        rows = slice(s * FFN_SUB, (s + 1) * FFN_SUB)
        h = h_ref[rows, :]
        n2 = _rms(h, n2w_ref[...]).astype(BF16)
        for f in range(D_FF // FF_CHUNK):
            fs = slice(f * FF_CHUNK, (f + 1) * FF_CHUNK)
            a = _dot(n2, wg_ref[:, fs])
            u = _dot(n2, wu_ref[:, fs])
            hid_ref[rows, fs] = (a * jax.nn.sigmoid(a) * u).astype(BF16)
        h2 = h + _dot(hid_ref[rows, :], wd_ref[...])
        o_ref[rows, :] = _rms(h2, fnw_ref[...])


def _const_spec(shape):
    nd = len(shape)
    return pl.BlockSpec(shape, lambda *_: (0,) * nd, pipeline_mode=pl.Buffered(1))


def _mixer(x, n1w, win, wgu, bgu, gnw, lnw, lnb, ws, bst, wout):
    bsz, seq, _ = x.shape
    assert seq % MIX_TILE == 0 and MIX_TILE % MIX_SUB == 0 and MIX_SUB % CUM_GROUP == 0
    consts = (n1w, win, wgu, bgu, gnw, lnw, lnb, ws, bst, wout)
    return pl.pallas_call(
        _mixer_kernel,
        grid=(bsz, seq // MIX_TILE),
        in_specs=[pl.BlockSpec((1, MIX_TILE, D_MODEL), lambda b, t: (b, t, 0))]
        + [_const_spec(a.shape) for a in consts],
        out_specs=pl.BlockSpec((1, MIX_TILE, D_MODEL), lambda b, t: (b, t, 0)),
        out_shape=jax.ShapeDtypeStruct(x.shape, F32),
        scratch_shapes=[
            pltpu.VMEM((GLA_WIDTH, GLA_KEY), F32),
            pltpu.VMEM((MIX_TILE, GLA_WIDTH), F32),
            pltpu.VMEM((MIX_TILE, D_MODEL), BF16),
        ],
        compiler_params=pltpu.CompilerParams(
            dimension_semantics=("parallel", "arbitrary"),
            vmem_limit_bytes=VMEM_LIMIT),
        name="mixer",
    )(x, *consts)


def _ffn(h, n2w, wg, wu, wd, fnw):
    m = h.shape[0]
    assert m % FFN_TILE == 0
    consts = (n2w, wg, wu, wd, fnw)
    return pl.pallas_call(
        _ffn_kernel,
        grid=(m // FFN_TILE,),
        in_specs=[pl.BlockSpec((FFN_TILE, D_MODEL), lambda i: (i, 0))]
        + [_const_spec(a.shape) for a in consts],
        out_specs=pl.BlockSpec((FFN_TILE, D_MODEL), lambda i: (i, 0)),
        out_shape=jax.ShapeDtypeStruct(h.shape, F32),
        scratch_shapes=[pltpu.VMEM((FFN_TILE, D_FF), BF16)],
        compiler_params=pltpu.CompilerParams(
            dimension_semantics=("parallel",),
            vmem_limit_bytes=VMEM_LIMIT),
        name="ffn",
    )(h, *consts)


def kernel(x, norm1_w, w_in, w_gate_up, b_gate_up, gla_norm_w, sg_ln_w, sg_ln_b, sg_w_s, sg_b_s,
           w_out, norm2_w, w_ffn_gate, w_ffn_up, w_ffn_down, final_norm_w):
    bsz, seq, _ = x.shape
    assert w_in.shape[0] == 1, "the final norm is fused into the (single) layer's ffn call"
    wi = w_in[0]
    o_v = GLA_KEY + GLA_KEY + GLA_WIDTH
    o_glr = o_v + GATE_RANK
    win = jnp.concatenate(
        [wi[:, :o_v], wi[:, o_glr:], wi[:, o_v:o_glr],
         jnp.zeros((D_MODEL, GLR_PAD - GATE_RANK), wi.dtype)], axis=1).astype(BF16)
    wgu = jnp.concatenate(
        [w_gate_up[0], jnp.zeros((GLR_PAD - GATE_RANK, GLA_KEY), w_gate_up.dtype)],
        axis=0).astype(BF16)
    h = _mixer(
        x, norm1_w[0][None, :], win, wgu, b_gate_up[0][None, :], gla_norm_w[0][None, :],
        sg_ln_w[0], sg_ln_b[0], sg_w_s[0].astype(BF16), jnp.transpose(sg_b_s[0]),
        w_out[0].astype(BF16))
    out = _ffn(h.reshape(bsz * seq, D_MODEL), norm2_w[0][None, :], w_ffn_gate[0].astype(BF16),
               w_ffn_up[0].astype(BF16), w_ffn_down[0].astype(BF16), final_norm_w[None, :])
    return out.reshape(bsz, seq, D_MODEL)
```

```python
import jax
import jax.numpy as jnp
from jax import lax
from jax.experimental import pallas as pl
from jax.experimental.pallas import tpu as pltpu

F32 = jnp.float32
BF16 = jnp.bfloat16

D_MODEL = 1024
GLA_HEADS = 4
GLA_DK = 64
GLA_DV = 128
GLA_KEY = GLA_HEADS * GLA_DK
GLA_WIDTH = GLA_HEADS * GLA_DV
GATE_RANK = 16
GATE_NORMALIZER = 16.0
GLA_CHUNK = 64
SG_GROUPS = 4
SG_CH = 128
SG_WIDTH = SG_GROUPS * SG_CH
SG_CHUNK = 128
D_FF = 2816
NORM_EPS = 1e-5

LANES = 128
GLR_PAD = LANES

C_Q = 0
C_K = C_Q + GLA_KEY
C_V = C_K + GLA_KEY
C_GOUT = C_V + GLA_WIDTH
C_SU = C_GOUT + GLA_WIDTH
C_SV = C_SU + SG_WIDTH
C_GLR = C_SV + SG_WIDTH
IN_COLS_PAD = C_GLR + GLR_PAD

PAIR = 2 * GLA_CHUNK
HEADS_PER_GROUP = LANES // GLA_DK
LANE_GROUPS = GLA_HEADS // HEADS_PER_GROUP
ST_ROWS = HEADS_PER_GROUP * GLA_DV
PROJ_COLS = 256

MIX_TILE = 1024
MIX_SUB = 512
FFN_TILE = 1024
FFN_SUB = 512
FF_CHUNK = 256
VMEM_LIMIT = 56 * 1024 * 1024


def _rms(x, w):
    ms = jnp.mean(x * x, axis=-1, keepdims=True)
    return x * lax.rsqrt(ms + NORM_EPS) * w


def _gelu(x):
    return 0.5 * x * (1.0 + lax.erf(x * (0.5 ** 0.5)))


def _dot(a, b):
    return jnp.dot(a, b, preferred_element_type=F32)


def _dot_nt(a, b):
    return lax.dot_general(a, b, (((1,), (1,)), ((), ())), preferred_element_type=F32)


def _mixer_kernel(x_ref, n1w_ref, win_ref, wgu_ref, bgu_ref, gnw_ref, lnw_ref, lnb_ref,
                  ws_ref, bst_ref, wout_ref, o_ref, st_ref, ogla_ref, mix_ref):
    @pl.when(pl.program_id(1) == 0)
    def _():
        st_ref[...] = jnp.zeros_like(st_ref)

    st = [st_ref[g] for g in range(LANE_GROUPS)]
    for s in range(x_ref.shape[1] // MIX_SUB):
        st = _mixer_sub_tile(s * MIX_SUB, st, x_ref, n1w_ref, win_ref, wgu_ref, bgu_ref, gnw_ref,
                             lnw_ref, lnb_ref, ws_ref, bst_ref, wout_ref, o_ref, ogla_ref, mix_ref)
    for g in range(LANE_GROUPS):
        st_ref[g] = st[g]


def _mixer_sub_tile(base, st, x_ref, n1w_ref, win_ref, wgu_ref, bgu_ref, gnw_ref, lnw_ref, lnb_ref,
                    ws_ref, bst_ref, wout_ref, o_ref, ogla_ref, mix_ref):
    tt = MIX_SUB
    n_pairs = tt // PAIR
    per = PROJ_COLS // LANES
    rows = slice(base, base + tt)

    x = x_ref[0, rows, :]
    n = _rms(x, n1w_ref[...]).astype(BF16)

    def proj(lo, width):
        return _dot(n, win_ref[:, lo:lo + width])

    def proj_wide(lo, j):
        return proj(lo + j * PROJ_COLS, PROJ_COLS)

    def cols(pieces, g):
        return pieces[g // per][:, (g % per) * LANES:(g % per + 1) * LANES]

    glr = proj(C_GLR, GLR_PAD)
    sv = [proj_wide(C_SV, 0)]
    gk = _dot(glr.astype(BF16), wgu_ref[...]) + bgu_ref[...]
    sv.append(proj_wide(C_SV, 1))
    log_a = (jnp.minimum(gk, 0.0) - jnp.log1p(jnp.exp(-jnp.abs(gk)))) * (1.0 / GATE_NORMALIZER)
    k = proj(C_K, GLA_KEY)
    q = proj(C_Q, GLA_KEY)

    la_hi = log_a.astype(BF16)
    la_lo = (log_a - la_hi.astype(F32)).astype(BF16)
    r = lax.broadcasted_iota(jnp.int32, (PAIR, 2 * PAIR), 0)
    c = lax.broadcasted_iota(jnp.int32, (PAIR, 2 * PAIR), 1) % PAIR
    tri2 = (r >= c).astype(BF16)
    b_parts = []
    for p in range(n_pairs):
        psl = slice(p * PAIR, (p + 1) * PAIR)
        b_parts.append(_dot(tri2, jnp.concatenate([la_hi[psl], la_lo[psl]], axis=0)))
    v = proj(C_V, GLA_WIDTH)

    y = []
    for g in range(SG_GROUPS):
        v_g = _gelu(cols(sv, g))
        mu = jnp.mean(v_g, axis=-1, keepdims=True)
        d = v_g - mu
        var = jnp.mean(d * d, axis=-1, keepdims=True)
        y.append((d * lax.rsqrt(var + NORM_EPS) * lnw_ref[g:g + 1, :]
                  + lnb_ref[g:g + 1, :]).astype(BF16))

    lane_head = lax.broadcasted_iota(jnp.int32, (PAIR, GLA_KEY), 1) // GLA_DK
    pr = lax.broadcasted_iota(jnp.int32, (GLA_HEADS * PAIR, PAIR), 0) % PAIR
    pc = lax.broadcasted_iota(jnp.int32, (GLA_HEADS * PAIR, PAIR), 1)
    causal = pr >= pc
    sr = lax.broadcasted_iota(jnp.int32, (ST_ROWS, LANES), 0) // GLA_DV
    sc = lax.broadcasted_iota(jnp.int32, (ST_ROWS, LANES), 1) // GLA_DK
    head_diag = sr == sc

    def gla_scores(p):
        psl = slice(p * PAIR, (p + 1) * PAIR)
        b = b_parts[p]
        b_mid = b[GLA_CHUNK - 1:GLA_CHUNK]
        b_end = b[PAIR - 1:PAIR]
        qs = q[psl] * (GLA_DK ** -0.5)
        kp = k[psl]
        q_rel = qs * jnp.exp(b - b_mid)
        k_rel = (kp * jnp.exp(b_mid - b)).astype(BF16)
        q_in = (qs * jnp.exp(b)).astype(BF16)
        k_out = (kp * jnp.exp(b_end - b)).astype(BF16)
        q_stack = jnp.concatenate(
            [jnp.where(lane_head == h, q_rel, 0.0) for h in range(GLA_HEADS)], axis=0).astype(BF16)
        probs = jnp.where(causal, _dot_nt(q_stack, k_rel), 0.0).astype(BF16)
        return probs, q_in, k_out, jnp.exp(b_end)

    def gla_apply(p, st, probs, q_in, k_out, decay):
        psl = slice(p * PAIR, (p + 1) * PAIR)
        vp = v[psl]
        vp_b = vp.astype(BF16)
        o_intra = jnp.concatenate(
            [_dot(probs[h * PAIR:(h + 1) * PAIR], vp_b[:, h * GLA_DV:(h + 1) * GLA_DV])
             for h in range(GLA_HEADS)], axis=1)
        vt = vp.T.astype(BF16)
        o_inter, st_new = [], []
        for g in range(LANE_GROUPS):
            lanes = slice(g * LANES, (g + 1) * LANES)
            o_inter.append(_dot_nt(q_in[:, lanes], st[g].astype(BF16)))
            u_t = _dot(vt[g * ST_ROWS:(g + 1) * ST_ROWS], k_out[:, lanes])
            st_new.append(st[g] * decay[:, lanes] + jnp.where(head_diag, u_t, 0.0))
        ogla_ref[base + p * PAIR:base + (p + 1) * PAIR, :] = (
            o_intra + jnp.concatenate(o_inter, axis=1))
        return st_new

    wr = lax.broadcasted_iota(jnp.int32, (SG_CHUNK, SG_CHUNK), 0)
    wc = lax.broadcasted_iota(jnp.int32, (SG_CHUNK, SG_CHUNK), 1)

    def sg_mix(su):
        for g in range(SG_GROUPS):
            u_g = _gelu(cols(su, g))
            w_g = jnp.where(wr >= wc, ws_ref[g], jnp.zeros((), BF16))
            bias = bst_ref[:, g:g + 1]
            for p in range(tt // SG_CHUNK):
                rs = slice(p * SG_CHUNK, (p + 1) * SG_CHUNK)
                mixed = _dot(w_g, y[g][rs]) + bias
                mix_ref[base + p * SG_CHUNK:base + (p + 1) * SG_CHUNK,
                        GLA_WIDTH + g * SG_CH:GLA_WIDTH + (g + 1) * SG_CH] = (
                            u_g[rs] * mixed).astype(BF16)

    su = [proj_wide(C_SU, j) for j in range(SG_WIDTH // PROJ_COLS)]
    go = []
    acc = []
    fills = [lambda: sg_mix(su),
             lambda: go.append(proj_wide(C_GOUT, 0)),
             lambda: go.append(proj_wide(C_GOUT, 1)),
             lambda: acc.append(x + _dot(mix_ref[rows, GLA_WIDTH:], wout_ref[GLA_WIDTH:, :]))]
    assert len(fills) == n_pairs
    staged = gla_scores(0)
    for p in range(n_pairs):
        fills[p]()
        nxt = gla_scores(p + 1) if p + 1 < n_pairs else None
        st = gla_apply(p, st, *staged)
        staged = nxt

    acc = acc[0]
    for h in range(GLA_HEADS):
        hs = slice(h * GLA_DV, (h + 1) * GLA_DV)
        o_h = _rms(ogla_ref[rows, hs], gnw_ref[...])
        g_h = cols(go, h)
        mix_ref[rows, hs] = (o_h * (g_h * jax.nn.sigmoid(g_h))).astype(BF16)
    o_ref[0, rows, :] = acc + _dot(mix_ref[rows, :GLA_WIDTH], wout_ref[:GLA_WIDTH, :])
    return st


def _ffn_kernel(h_ref, n2w_ref, wg_ref, wu_ref, wd_ref, fnw_ref, o_ref, hid_ref):
    for s in range(h_ref.shape[0] // FFN_SUB):
        rows = slice(s * FFN_SUB, (s + 1) * FFN_SUB)
        h = h_ref[rows, :]
        n2 = _rms(h, n2w_ref[...]).astype(BF16)
        for f in range(D_FF // FF_CHUNK):
            fs = slice(f * FF_CHUNK, (f + 1) * FF_CHUNK)
            a = _dot(n2, wg_ref[:, fs])
            u = _dot(n2, wu_ref[:, fs])
            hid_ref[rows, fs] = (a * jax.nn.sigmoid(a) * u).astype(BF16)
        h2 = h + _dot(hid_ref[rows, :], wd_ref[...])
        o_ref[rows, :] = _rms(h2, fnw_ref[...])


def _const_spec(shape):
    nd = len(shape)
    return pl.BlockSpec(shape, lambda *_: (0,) * nd, pipeline_mode=pl.Buffered(1))


def _mixer(x, n1w, win, wgu, bgu, gnw, lnw, lnb, ws, bst, wout):
    bsz, seq, _ = x.shape
    assert seq % MIX_TILE == 0 and MIX_TILE % MIX_SUB == 0 and MIX_SUB % PAIR == 0
    consts = (n1w, win, wgu, bgu, gnw, lnw, lnb, ws, bst, wout)
    return pl.pallas_call(
        _mixer_kernel,
        grid=(bsz, seq // MIX_TILE),
        in_specs=[pl.BlockSpec((1, MIX_TILE, D_MODEL), lambda b, t: (b, t, 0))]
        + [_const_spec(a.shape) for a in consts],
        out_specs=pl.BlockSpec((1, MIX_TILE, D_MODEL), lambda b, t: (b, t, 0)),
        out_shape=jax.ShapeDtypeStruct(x.shape, F32),
        scratch_shapes=[
            pltpu.VMEM((LANE_GROUPS, ST_ROWS, LANES), F32),
            pltpu.VMEM((MIX_TILE, GLA_WIDTH), F32),
            pltpu.VMEM((MIX_TILE, D_MODEL), BF16),
        ],
        compiler_params=pltpu.CompilerParams(
            dimension_semantics=("parallel", "arbitrary"),
            vmem_limit_bytes=VMEM_LIMIT),
        name="mixer",
    )(x, *consts)


def _ffn(h, n2w, wg, wu, wd, fnw):
    m = h.shape[0]
    assert m % FFN_TILE == 0
    consts = (n2w, wg, wu, wd, fnw)
    return pl.pallas_call(
        _ffn_kernel,
        grid=(m // FFN_TILE,),
        in_specs=[pl.BlockSpec((FFN_TILE, D_MODEL), lambda i: (i, 0))]
        + [_const_spec(a.shape) for a in consts],
        out_specs=pl.BlockSpec((FFN_TILE, D_MODEL), lambda i: (i, 0)),
        out_shape=jax.ShapeDtypeStruct(h.shape, F32),
        scratch_shapes=[pltpu.VMEM((FFN_TILE, D_FF), BF16)],
        compiler_params=pltpu.CompilerParams(
            dimension_semantics=("parallel",),
            vmem_limit_bytes=VMEM_LIMIT),
        name="ffn",
    )(h, *consts)


def kernel(x, norm1_w, w_in, w_gate_up, b_gate_up, gla_norm_w, sg_ln_w, sg_ln_b, sg_w_s, sg_b_s,
           w_out, norm2_w, w_ffn_gate, w_ffn_up, w_ffn_down, final_norm_w):
    bsz, seq, _ = x.shape
    assert w_in.shape[0] == 1, "the final norm is fused into the (single) layer's ffn call"
    wi = w_in[0]
    o_v = GLA_KEY + GLA_KEY + GLA_WIDTH
    o_glr = o_v + GATE_RANK
    win = jnp.concatenate(
        [wi[:, :o_v], wi[:, o_glr:], wi[:, o_v:o_glr],
         jnp.zeros((D_MODEL, GLR_PAD - GATE_RANK), wi.dtype)], axis=1).astype(BF16)
    wgu = jnp.concatenate(
        [w_gate_up[0], jnp.zeros((GLR_PAD - GATE_RANK, GLA_KEY), w_gate_up.dtype)],
        axis=0).astype(BF16)
    h = _mixer(
        x, norm1_w[0][None, :], win, wgu, b_gate_up[0][None, :], gla_norm_w[0][None, :],
        sg_ln_w[0], sg_ln_b[0], sg_w_s[0].astype(BF16), jnp.transpose(sg_b_s[0]),
        w_out[0].astype(BF16))
    out = _ffn(h.reshape(bsz * seq, D_MODEL), norm2_w[0][None, :], w_ffn_gate[0].astype(BF16),
               w_ffn_up[0].astype(BF16), w_ffn_down[0].astype(BF16), final_norm_w[None, :])
    return out.reshape(bsz, seq, D_MODEL)
```

```python
import jax
import jax.numpy as jnp
from jax import lax
from jax.experimental import pallas as pl
from jax.experimental.pallas import tpu as pltpu

F32 = jnp.float32
BF16 = jnp.bfloat16

D_MODEL = 1024
GLA_HEADS = 4
GLA_DK = 64
GLA_DV = 128
GLA_KEY = GLA_HEADS * GLA_DK
GLA_WIDTH = GLA_HEADS * GLA_DV
GATE_RANK = 16
GATE_NORMALIZER = 16.0
GLA_CHUNK = 64
SG_GROUPS = 4
SG_CH = 128
SG_WIDTH = SG_GROUPS * SG_CH
SG_CHUNK = 128
D_FF = 2816
NORM_EPS = 1e-5

LANES = 128
BF16_SUBLANES = 16
GLR_PAD = LANES

C_Q = 0
C_K = C_Q + GLA_KEY
C_V = C_K + GLA_KEY
C_GOUT = C_V + GLA_WIDTH
C_SU = C_GOUT + GLA_WIDTH
C_SV = C_SU + SG_WIDTH
C_GLR = C_SV + SG_WIDTH
IN_COLS_PAD = C_GLR + GLR_PAD

PAIR = 2 * GLA_CHUNK
HEADS_PER_GROUP = LANES // GLA_DK
LANE_GROUPS = GLA_HEADS // HEADS_PER_GROUP
ST_ROWS = HEADS_PER_GROUP * GLA_DV
PROJ_COLS = 256

MIX_TILE = 1024
MIX_SUB = 512
FFN_TILE = 1024
FFN_SUB = 512
FF_CHUNK = 256
FFN_LEAD = 1
VMEM_LIMIT = 56 * 1024 * 1024


def _rms(x, w):
    ms = jnp.mean(x * x, axis=-1, keepdims=True)
    return x * lax.rsqrt(ms + NORM_EPS) * w


def _gelu(x):
    return 0.5 * x * (1.0 + lax.erf(x * (0.5 ** 0.5)))


def _dot(a, b):
    return jnp.dot(a, b, preferred_element_type=F32)


def _dot_nt(a, b):
    return lax.dot_general(a, b, (((1,), (1,)), ((), ())), preferred_element_type=F32)


def _mixer_kernel(x_ref, n1w_ref, win_ref, wgu_ref, bgu_ref, gnw_ref, lnw_ref, lnb_ref,
                  ws_ref, bst_ref, wout_ref, f0_ref, f1_ref, f2_ref,
                  o_ref, f0b_ref, f1b_ref, f2b_ref, st_ref, ogla_ref, mix_ref):
    @pl.when(pl.program_id(1) == 0)
    def _():
        st_ref[...] = jnp.zeros_like(st_ref)

    for src, dst in ((f0_ref, f0b_ref), (f1_ref, f1b_ref), (f2_ref, f2b_ref)):
        dst[...] = src[...].astype(BF16)

    st = [st_ref[g] for g in range(LANE_GROUPS)]
    for s in range(x_ref.shape[1] // MIX_SUB):
        st = _mixer_sub_tile(s * MIX_SUB, st, x_ref, n1w_ref, win_ref, wgu_ref, bgu_ref, gnw_ref,
                             lnw_ref, lnb_ref, ws_ref, bst_ref, wout_ref, o_ref, ogla_ref, mix_ref)
    for g in range(LANE_GROUPS):
        st_ref[g] = st[g]


def _mixer_sub_tile(base, st, x_ref, n1w_ref, win_ref, wgu_ref, bgu_ref, gnw_ref, lnw_ref, lnb_ref,
                    ws_ref, bst_ref, wout_ref, o_ref, ogla_ref, mix_ref):
    tt = MIX_SUB
    n_pairs = tt // PAIR
    per = PROJ_COLS // LANES
    rows = slice(base, base + tt)

    x = x_ref[0, rows, :]
    n = _rms(x, n1w_ref[...]).astype(BF16)

    def proj(lo, width):
        return _dot(n, win_ref[:, lo:lo + width])

    def proj_wide(lo, j):
        return proj(lo + j * PROJ_COLS, PROJ_COLS)

    def cols(pieces, g):
        return pieces[g // per][:, (g % per) * LANES:(g % per + 1) * LANES]

    glr = proj(C_GLR, GLR_PAD)
    sv = [proj_wide(C_SV, 0)]
    gk = _dot(glr.astype(BF16), wgu_ref[...]) + bgu_ref[...]
    sv.append(proj_wide(C_SV, 1))
    log_a = (jnp.minimum(gk, 0.0) - jnp.log1p(jnp.exp(-jnp.abs(gk)))) * (1.0 / GATE_NORMALIZER)
    k = proj(C_K, GLA_KEY)
    q = proj(C_Q, GLA_KEY)

    la_hi = log_a.astype(BF16)
    la_lo = (log_a - la_hi.astype(F32)).astype(BF16)
    r = lax.broadcasted_iota(jnp.int32, (PAIR, 2 * PAIR), 0)
    c = lax.broadcasted_iota(jnp.int32, (PAIR, 2 * PAIR), 1) % PAIR
    tri2 = (r >= c).astype(BF16)
    b_parts = []
    for p in range(n_pairs):
        psl = slice(p * PAIR, (p + 1) * PAIR)
        b_parts.append(_dot(tri2, jnp.concatenate([la_hi[psl], la_lo[psl]], axis=0)))
    v = proj(C_V, GLA_WIDTH)

    y = []
    for g in range(SG_GROUPS):
        v_g = _gelu(cols(sv, g))
        mu = jnp.mean(v_g, axis=-1, keepdims=True)
        d = v_g - mu
        var = jnp.mean(d * d, axis=-1, keepdims=True)
        y.append((d * lax.rsqrt(var + NORM_EPS) * lnw_ref[g:g + 1, :]
                  + lnb_ref[g:g + 1, :]).astype(BF16))

    lane_head = lax.broadcasted_iota(jnp.int32, (PAIR, GLA_KEY), 1) // GLA_DK
    pr = lax.broadcasted_iota(jnp.int32, (GLA_HEADS * PAIR, PAIR), 0) % PAIR
    pc = lax.broadcasted_iota(jnp.int32, (GLA_HEADS * PAIR, PAIR), 1)
    causal = pr >= pc
    sr = lax.broadcasted_iota(jnp.int32, (ST_ROWS, LANES), 0) // GLA_DV
    sc = lax.broadcasted_iota(jnp.int32, (ST_ROWS, LANES), 1) // GLA_DK
    head_diag = sr == sc

    def gla_scores(p):
        psl = slice(p * PAIR, (p + 1) * PAIR)
        b = b_parts[p]
        b_mid = b[GLA_CHUNK - 1:GLA_CHUNK]
        b_end = b[PAIR - 1:PAIR]
        qs = q[psl] * (GLA_DK ** -0.5)
        kp = k[psl]
        q_rel = qs * jnp.exp(b - b_mid)
        k_rel = (kp * jnp.exp(b_mid - b)).astype(BF16)
        q_in = (qs * jnp.exp(b)).astype(BF16)
        k_out = (kp * jnp.exp(b_end - b)).astype(BF16)
        q_stack = jnp.concatenate(
            [jnp.where(lane_head == h, q_rel, 0.0) for h in range(GLA_HEADS)], axis=0).astype(BF16)
        probs = jnp.where(causal, _dot_nt(q_stack, k_rel), 0.0).astype(BF16)
        return probs, q_in, k_out, jnp.exp(b_end)

    def gla_apply(p, st, probs, q_in, k_out, decay):
        psl = slice(p * PAIR, (p + 1) * PAIR)
        vp = v[psl]
        vp_b = vp.astype(BF16)
        o_intra = jnp.concatenate(
            [_dot(probs[h * PAIR:(h + 1) * PAIR], vp_b[:, h * GLA_DV:(h + 1) * GLA_DV])
             for h in range(GLA_HEADS)], axis=1)
        vt = vp.T.astype(BF16)
        o_inter, st_new = [], []
        for g in range(LANE_GROUPS):
            lanes = slice(g * LANES, (g + 1) * LANES)
            o_inter.append(_dot_nt(q_in[:, lanes], st[g].astype(BF16)))
            u_t = _dot(vt[g * ST_ROWS:(g + 1) * ST_ROWS], k_out[:, lanes])
            st_new.append(st[g] * decay[:, lanes] + jnp.where(head_diag, u_t, 0.0))
        ogla_ref[base + p * PAIR:base + (p + 1) * PAIR, :] = (
            o_intra + jnp.concatenate(o_inter, axis=1))
        return st_new

    wr = lax.broadcasted_iota(jnp.int32, (SG_CHUNK, SG_CHUNK), 0)
    wc = lax.broadcasted_iota(jnp.int32, (SG_CHUNK, SG_CHUNK), 1)

    def sg_mix(su):
        for g in range(SG_GROUPS):
            u_g = _gelu(cols(su, g))
            w_g = jnp.where(wr >= wc, ws_ref[g], jnp.zeros((), BF16))
            bias = bst_ref[:, g:g + 1]
            for p in range(tt // SG_CHUNK):
                rs = slice(p * SG_CHUNK, (p + 1) * SG_CHUNK)
                mixed = _dot(w_g, y[g][rs]) + bias
                mix_ref[base + p * SG_CHUNK:base + (p + 1) * SG_CHUNK,
                        GLA_WIDTH + g * SG_CH:GLA_WIDTH + (g + 1) * SG_CH] = (
                            u_g[rs] * mixed).astype(BF16)

    su = [proj_wide(C_SU, j) for j in range(SG_WIDTH // PROJ_COLS)]
    go = []
    acc = []
    fills = [lambda: sg_mix(su),
             lambda: go.append(proj_wide(C_GOUT, 0)),
             lambda: go.append(proj_wide(C_GOUT, 1)),
             lambda: acc.append(x + _dot(mix_ref[rows, GLA_WIDTH:], wout_ref[GLA_WIDTH:, :]))]
    assert len(fills) == n_pairs
    staged = gla_scores(0)
    for p in range(n_pairs):
        fills[p]()
        nxt = gla_scores(p + 1) if p + 1 < n_pairs else None
        st = gla_apply(p, st, *staged)
        staged = nxt

    acc = acc[0]
    for h in range(GLA_HEADS):
        hs = slice(h * GLA_DV, (h + 1) * GLA_DV)
        o_h = _rms(ogla_ref[rows, hs], gnw_ref[...])
        g_h = cols(go, h)
        mix_ref[rows, hs] = (o_h * (g_h * jax.nn.sigmoid(g_h))).astype(BF16)
    o_ref[0, rows, :] = acc + _dot(mix_ref[rows, :GLA_WIDTH], wout_ref[:GLA_WIDTH, :])
    return st


def _ffn_kernel(h_ref, n2w_ref, wg_ref, wu_ref, wd_ref, fnw_ref, o_ref, hid_ref):
    n_sub = h_ref.shape[0] // FFN_SUB
    n_chunks = D_FF // FF_CHUNK
    subs = [slice(s * FFN_SUB, (s + 1) * FFN_SUB) for s in range(n_sub)]
    n2 = [None] * n_sub

    def gate_up(s, f):
        if n2[s] is None:
            n2[s] = _rms(h_ref[subs[s], :], n2w_ref[...]).astype(BF16)
        fs = slice(f * FF_CHUNK, (f + 1) * FF_CHUNK)
        a = _dot(n2[s], wg_ref[:, fs])
        u = _dot(n2[s], wu_ref[:, fs])
        hid_ref[subs[s], fs] = (a * jax.nn.sigmoid(a) * u).astype(BF16)

    def down(s):
        h2 = h_ref[subs[s], :] + _dot(hid_ref[subs[s], :], wd_ref[...])
        o_ref[subs[s], :] = _rms(h2, fnw_ref[...])

    for s in range(n_sub):
        for f in range(FFN_LEAD if s else 0, n_chunks):
            gate_up(s, f)
        if s + 1 < n_sub:
            for f in range(FFN_LEAD):
                gate_up(s + 1, f)
        down(s)


def _const_spec(shape):
    nd = len(shape)
    return pl.BlockSpec(shape, lambda *_: (0,) * nd, pipeline_mode=pl.Buffered(1))


def _mixer(x, n1w, win, wgu, bgu, gnw, lnw, lnb, ws, bst, wout, ffn_ws):
    bsz, seq, _ = x.shape
    assert seq % MIX_TILE == 0 and MIX_TILE % MIX_SUB == 0 and MIX_SUB % PAIR == 0
    n_t = seq // MIX_TILE
    consts = (n1w, win, wgu, bgu, gnw, lnw, lnb, ws, bst, wout)

    def slab_spec(w):
        rows, width = w.shape
        slab = next(r for r in range(BF16_SUBLANES, rows + 1, BF16_SUBLANES)
                    if rows % r == 0 and r * bsz * n_t >= rows)
        last = rows // slab - 1
        return pl.BlockSpec((slab, width), lambda b, t: (jnp.minimum(b * n_t + t, last), 0))

    slab_specs = [slab_spec(w) for w in ffn_ws]
    tile_spec = pl.BlockSpec((1, MIX_TILE, D_MODEL), lambda b, t: (b, t, 0))
    h, *narrowed = pl.pallas_call(
        _mixer_kernel,
        grid=(bsz, n_t),
        in_specs=[tile_spec] + [_const_spec(a.shape) for a in consts] + slab_specs,
        out_specs=[tile_spec] + slab_specs,
        out_shape=[jax.ShapeDtypeStruct(x.shape, F32)]
        + [jax.ShapeDtypeStruct(w.shape, BF16) for w in ffn_ws],
        scratch_shapes=[
            pltpu.VMEM((LANE_GROUPS, ST_ROWS, LANES), F32),
            pltpu.VMEM((MIX_TILE, GLA_WIDTH), F32),
            pltpu.VMEM((MIX_TILE, D_MODEL), BF16),
        ],
        compiler_params=pltpu.CompilerParams(
            dimension_semantics=("arbitrary", "arbitrary"),
            vmem_limit_bytes=VMEM_LIMIT),
        name="mixer",
    )(x, *consts, *ffn_ws)
    return h, narrowed


def _ffn(h, n2w, wg, wu, wd, fnw):
    m = h.shape[0]
    assert m % FFN_TILE == 0
    consts = (n2w, wg, wu, wd, fnw)
    return pl.pallas_call(
        _ffn_kernel,
        grid=(m // FFN_TILE,),
        in_specs=[pl.BlockSpec((FFN_TILE, D_MODEL), lambda i: (i, 0))]
        + [_const_spec(a.shape) for a in consts],
        out_specs=pl.BlockSpec((FFN_TILE, D_MODEL), lambda i: (i, 0)),
        out_shape=jax.ShapeDtypeStruct(h.shape, F32),
        scratch_shapes=[pltpu.VMEM((FFN_TILE, D_FF), BF16)],
        compiler_params=pltpu.CompilerParams(
            dimension_semantics=("parallel",),
            vmem_limit_bytes=VMEM_LIMIT),
        name="ffn",
    )(h, *consts)


def kernel(x, norm1_w, w_in, w_gate_up, b_gate_up, gla_norm_w, sg_ln_w, sg_ln_b, sg_w_s, sg_b_s,
           w_out, norm2_w, w_ffn_gate, w_ffn_up, w_ffn_down, final_norm_w):
    bsz, seq, _ = x.shape
    assert w_in.shape[0] == 1, "the final norm is fused into the (single) layer's ffn call"
    wi = w_in[0]
    o_v = GLA_KEY + GLA_KEY + GLA_WIDTH
    o_glr = o_v + GATE_RANK
    win = jnp.concatenate(
        [wi[:, :o_v], wi[:, o_glr:], wi[:, o_v:o_glr],
         jnp.zeros((D_MODEL, GLR_PAD - GATE_RANK), wi.dtype)], axis=1).astype(BF16)
    wgu = jnp.concatenate(
        [w_gate_up[0], jnp.zeros((GLR_PAD - GATE_RANK, GLA_KEY), w_gate_up.dtype)],
        axis=0).astype(BF16)
    h, (wg, wu, wd) = _mixer(
        x, norm1_w[0][None, :], win, wgu, b_gate_up[0][None, :], gla_norm_w[0][None, :],
        sg_ln_w[0], sg_ln_b[0], sg_w_s[0].astype(BF16), jnp.transpose(sg_b_s[0]),
        w_out[0].astype(BF16), (w_ffn_gate[0], w_ffn_up[0], w_ffn_down[0]))
    out = _ffn(h.reshape(bsz * seq, D_MODEL), norm2_w[0][None, :], wg, wu, wd,
               final_norm_w[None, :])
    return out.reshape(bsz, seq, D_MODEL)
```

```python
import jax
import jax.numpy as jnp
from jax import lax
from jax.experimental import pallas as pl
from jax.experimental.pallas import tpu as pltpu

F32 = jnp.float32
BF16 = jnp.bfloat16

D_MODEL = 1024
GLA_HEADS = 4
GLA_DK = 64
GLA_DV = 128
GLA_KEY = GLA_HEADS * GLA_DK
GLA_WIDTH = GLA_HEADS * GLA_DV
GATE_RANK = 16
GATE_NORMALIZER = 16.0
GLA_CHUNK = 64
SG_GROUPS = 4
SG_CH = 128
SG_WIDTH = SG_GROUPS * SG_CH
SG_CHUNK = 128
D_FF = 2816
NORM_EPS = 1e-5

LANES = 128
BF16_SUBLANES = 16
GLR_PAD = LANES

C_Q = 0
C_K = C_Q + GLA_KEY
C_V = C_K + GLA_KEY
C_GLR = C_V + GLA_WIDTH
A_COLS = C_GLR + GLR_PAD
C_GOUT = A_COLS
C_SU = C_GOUT + GLA_WIDTH
C_SV = C_SU + SG_WIDTH

PAIR = 2 * GLA_CHUNK
HEADS_PER_GROUP = LANES // GLA_DK
LANE_GROUPS = GLA_HEADS // HEADS_PER_GROUP
ST_ROWS = HEADS_PER_GROUP * GLA_DV
PROJ_COLS = 256

MIX_TILE = 1024
MIX_SUB = 512
FFN_TILE = 1024
FFN_SUB = 512
FF_CHUNK = 256
FFN_LEAD = 1
VMEM_LIMIT = 56 * 1024 * 1024


def _rms(x, w):
    ms = jnp.mean(x * x, axis=-1, keepdims=True)
    return x * lax.rsqrt(ms + NORM_EPS) * w


def _gelu(x):
    return 0.5 * x * (1.0 + lax.erf(x * (0.5 ** 0.5)))


def _dot(a, b):
    return jnp.dot(a, b, preferred_element_type=F32)


def _dot_nt(a, b):
    return lax.dot_general(a, b, (((1,), (1,)), ((), ())), preferred_element_type=F32)


def _mixer_kernel(x_ref, n1w_ref, wa_ref, wb_ref, wgu_ref, bgu_ref, gnw_ref, lnw_ref, lnb_ref,
                  ws_ref, bst_ref, wout_ref, f0_ref, f1_ref, f2_ref,
                  o_ref, f0b_ref, f1b_ref, f2b_ref, st_ref, ogla_ref, mix_ref, wfold_ref):
    @pl.when(pl.program_id(1) == 0)
    def _():
        st_ref[...] = jnp.zeros_like(st_ref)

    @pl.when((pl.program_id(0) == 0) & (pl.program_id(1) == 0))
    def _():
        wfold_ref[...] = _dot(wa_ref[:, C_GLR:C_GLR + GLR_PAD], wgu_ref[...]).astype(BF16)

    st = [st_ref[g] for g in range(LANE_GROUPS)]
    for s in range(x_ref.shape[1] // MIX_SUB):
        st = _mixer_sub_tile(s * MIX_SUB, st, x_ref, n1w_ref, wa_ref, wb_ref, wfold_ref, bgu_ref,
                             gnw_ref, lnw_ref, lnb_ref, ws_ref, bst_ref, wout_ref, o_ref, ogla_ref,
                             mix_ref)
    for g in range(LANE_GROUPS):
        st_ref[g] = st[g]

    for src, dst in ((f0_ref, f0b_ref), (f1_ref, f1b_ref), (f2_ref, f2b_ref)):
        dst[...] = src[...].astype(BF16)


def _mixer_sub_tile(base, st, x_ref, n1w_ref, wa_ref, wb_ref, wfold_ref, bgu_ref, gnw_ref, lnw_ref,
                    lnb_ref, ws_ref, bst_ref, wout_ref, o_ref, ogla_ref, mix_ref):
    tt = MIX_SUB
    n_pairs = tt // PAIR
    per = PROJ_COLS // LANES
    rows = slice(base, base + tt)

    x = x_ref[0, rows, :]
    n = _rms(x, n1w_ref[...]).astype(BF16)

    def proj(lo, width):
        if lo < A_COLS:
            return _dot(n, wa_ref[:, lo:lo + width])
        return _dot(n, wb_ref[:, lo - A_COLS:lo - A_COLS + width])

    def proj_wide(lo, j):
        return proj(lo + j * PROJ_COLS, PROJ_COLS)

    def cols(pieces, g):
        return pieces[g // per][:, (g % per) * LANES:(g % per + 1) * LANES]

    gk = _dot(n, wfold_ref[...]) + bgu_ref[...]
    sv = [proj_wide(C_SV, j) for j in range(SG_WIDTH // PROJ_COLS)]
    log_a = (jnp.minimum(gk, 0.0) - jnp.log1p(jnp.exp(-jnp.abs(gk)))) * (1.0 / GATE_NORMALIZER)
    k = proj(C_K, GLA_KEY)
    q = proj(C_Q, GLA_KEY)

    la_hi = log_a.astype(BF16)
    la_lo = (log_a - la_hi.astype(F32)).astype(BF16)
    r = lax.broadcasted_iota(jnp.int32, (PAIR, 2 * PAIR), 0)
    c = lax.broadcasted_iota(jnp.int32, (PAIR, 2 * PAIR), 1) % PAIR
    tri2 = (r >= c).astype(BF16)
    b_parts = []
    for p in range(n_pairs):
        psl = slice(p * PAIR, (p + 1) * PAIR)
        b_parts.append(_dot(tri2, jnp.concatenate([la_hi[psl], la_lo[psl]], axis=0)))
    v = proj(C_V, GLA_WIDTH)

    y = []
    for g in range(SG_GROUPS):
        v_g = _gelu(cols(sv, g))
        mu = jnp.mean(v_g, axis=-1, keepdims=True)
        d = v_g - mu
        var = jnp.mean(d * d, axis=-1, keepdims=True)
        y.append((d * lax.rsqrt(var + NORM_EPS) * lnw_ref[g:g + 1, :]
                  + lnb_ref[g:g + 1, :]).astype(BF16))

    lane_head = lax.broadcasted_iota(jnp.int32, (PAIR, GLA_KEY), 1) // GLA_DK
    pr = lax.broadcasted_iota(jnp.int32, (GLA_HEADS * PAIR, PAIR), 0) % PAIR
    pc = lax.broadcasted_iota(jnp.int32, (GLA_HEADS * PAIR, PAIR), 1)
    causal = pr >= pc
    sr = lax.broadcasted_iota(jnp.int32, (ST_ROWS, LANES), 0) // GLA_DV
    sc = lax.broadcasted_iota(jnp.int32, (ST_ROWS, LANES), 1) // GLA_DK
    head_diag = sr == sc

    def gla_scores(p):
        psl = slice(p * PAIR, (p + 1) * PAIR)
        b = b_parts[p]
        b_mid = b[GLA_CHUNK - 1:GLA_CHUNK]
        b_end = b[PAIR - 1:PAIR]
        qs = q[psl] * (GLA_DK ** -0.5)
        kp = k[psl]
        q_rel = qs * jnp.exp(b - b_mid)
        k_rel = (kp * jnp.exp(b_mid - b)).astype(BF16)
        q_in = (qs * jnp.exp(b)).astype(BF16)
        k_out = (kp * jnp.exp(b_end - b)).astype(BF16)
        q_stack = jnp.concatenate(
            [jnp.where(lane_head == h, q_rel, 0.0) for h in range(GLA_HEADS)], axis=0).astype(BF16)
        probs = jnp.where(causal, _dot_nt(q_stack, k_rel), 0.0).astype(BF16)
        return probs, q_in, k_out, jnp.exp(b_end)

    def gla_apply(p, st, probs, q_in, k_out, decay):
        psl = slice(p * PAIR, (p + 1) * PAIR)
        vp = v[psl]
        vp_b = vp.astype(BF16)
        o_intra = jnp.concatenate(
            [_dot(probs[h * PAIR:(h + 1) * PAIR], vp_b[:, h * GLA_DV:(h + 1) * GLA_DV])
             for h in range(GLA_HEADS)], axis=1)
        vt = vp.T.astype(BF16)
        o_inter, st_new = [], []
        for g in range(LANE_GROUPS):
            lanes = slice(g * LANES, (g + 1) * LANES)
            o_inter.append(_dot_nt(q_in[:, lanes], st[g].astype(BF16)))
            u_t = _dot(vt[g * ST_ROWS:(g + 1) * ST_ROWS], k_out[:, lanes])
            st_new.append(st[g] * decay[:, lanes] + jnp.where(head_diag, u_t, 0.0))
        ogla_ref[base + p * PAIR:base + (p + 1) * PAIR, :] = (
            o_intra + jnp.concatenate(o_inter, axis=1))
        return st_new

    wr = lax.broadcasted_iota(jnp.int32, (SG_CHUNK, SG_CHUNK), 0)
    wc = lax.broadcasted_iota(jnp.int32, (SG_CHUNK, SG_CHUNK), 1)

    def sg_mix(su):
        for g in range(SG_GROUPS):
            u_g = _gelu(cols(su, g))
            w_g = jnp.where(wr >= wc, ws_ref[g], jnp.zeros((), BF16))
            bias = bst_ref[:, g:g + 1]
            for p in range(tt // SG_CHUNK):
                rs = slice(p * SG_CHUNK, (p + 1) * SG_CHUNK)
                mixed = _dot(w_g, y[g][rs]) + bias
                mix_ref[base + p * SG_CHUNK:base + (p + 1) * SG_CHUNK,
                        GLA_WIDTH + g * SG_CH:GLA_WIDTH + (g + 1) * SG_CH] = (
                            u_g[rs] * mixed).astype(BF16)

    su = [proj_wide(C_SU, j) for j in range(SG_WIDTH // PROJ_COLS)]
    go = []
    acc = []
    fills = [lambda: sg_mix(su),
             lambda: go.append(proj_wide(C_GOUT, 0)),
             lambda: go.append(proj_wide(C_GOUT, 1)),
             lambda: acc.append(x + _dot(mix_ref[rows, GLA_WIDTH:], wout_ref[GLA_WIDTH:, :]))]
    assert len(fills) == n_pairs
    staged = gla_scores(0)
    for p in range(n_pairs):
        fills[p]()
        nxt = gla_scores(p + 1) if p + 1 < n_pairs else None
        st = gla_apply(p, st, *staged)
        staged = nxt

    acc = acc[0]
    for h in range(GLA_HEADS):
        hs = slice(h * GLA_DV, (h + 1) * GLA_DV)
        o_h = _rms(ogla_ref[rows, hs], gnw_ref[...])
        g_h = cols(go, h)
        mix_ref[rows, hs] = (o_h * (g_h * jax.nn.sigmoid(g_h))).astype(BF16)
    o_ref[0, rows, :] = acc + _dot(mix_ref[rows, :GLA_WIDTH], wout_ref[:GLA_WIDTH, :])
    return st


def _ffn_kernel(h_ref, n2w_ref, wg_ref, wu_ref, wd_ref, fnw_ref, o_ref, hid_ref):
    n_sub = h_ref.shape[0] // FFN_SUB
    n_chunks = D_FF // FF_CHUNK
    subs = [slice(s * FFN_SUB, (s + 1) * FFN_SUB) for s in range(n_sub)]
    n2 = [None] * n_sub

    def gate_up(s, f):
        if n2[s] is None:
            n2[s] = _rms(h_ref[subs[s], :], n2w_ref[...]).astype(BF16)
        fs = slice(f * FF_CHUNK, (f + 1) * FF_CHUNK)
        a = _dot(n2[s], wg_ref[:, fs])
        u = _dot(n2[s], wu_ref[:, fs])
        hid_ref[subs[s], fs] = (a * jax.nn.sigmoid(a) * u).astype(BF16)

    def down(s):
        h2 = h_ref[subs[s], :] + _dot(hid_ref[subs[s], :], wd_ref[...])
        o_ref[subs[s], :] = _rms(h2, fnw_ref[...])

    for s in range(n_sub):
        for f in range(FFN_LEAD if s else 0, n_chunks):
            gate_up(s, f)
        if s + 1 < n_sub:
            for f in range(FFN_LEAD):
                gate_up(s + 1, f)
        down(s)


def _const_spec(shape):
    nd = len(shape)
    return pl.BlockSpec(shape, lambda *_: (0,) * nd, pipeline_mode=pl.Buffered(1))


def _mixer(x, n1w, wa, wb, wgu, bgu, gnw, lnw, lnb, ws, bst, wout, ffn_ws):
    bsz, seq, _ = x.shape
    assert seq % MIX_TILE == 0 and MIX_TILE % MIX_SUB == 0 and MIX_SUB % PAIR == 0
    n_t = seq // MIX_TILE
    consts = (n1w, wa, wb, wgu, bgu, gnw, lnw, lnb, ws, bst, wout)

    def slab_spec(w):
        rows, width = w.shape
        slab = next(r for r in range(BF16_SUBLANES, rows + 1, BF16_SUBLANES)
                    if rows % r == 0 and r * bsz * n_t >= rows)
        last = rows // slab - 1
        return pl.BlockSpec((slab, width), lambda b, t: (jnp.minimum(b * n_t + t, last), 0))

    slab_specs = [slab_spec(w) for w in ffn_ws]
    tile_spec = pl.BlockSpec((1, MIX_TILE, D_MODEL), lambda b, t: (b, t, 0))
    h, *narrowed = pl.pallas_call(
        _mixer_kernel,
        grid=(bsz, n_t),
        in_specs=[tile_spec] + [_const_spec(a.shape) for a in consts] + slab_specs,
        out_specs=[tile_spec] + slab_specs,
        out_shape=[jax.ShapeDtypeStruct(x.shape, F32)]
        + [jax.ShapeDtypeStruct(w.shape, BF16) for w in ffn_ws],
        scratch_shapes=[
            pltpu.VMEM((LANE_GROUPS, ST_ROWS, LANES), F32),
            pltpu.VMEM((MIX_TILE, GLA_WIDTH), F32),
            pltpu.VMEM((MIX_TILE, D_MODEL), BF16),
            pltpu.VMEM((D_MODEL, GLA_KEY), BF16),
        ],
        compiler_params=pltpu.CompilerParams(
            dimension_semantics=("arbitrary", "arbitrary"),
            vmem_limit_bytes=VMEM_LIMIT),
        name="mixer",
    )(x, *consts, *ffn_ws)
    return h, narrowed


def _ffn(h, n2w, wg, wu, wd, fnw):
    m = h.shape[0]
    assert m % FFN_TILE == 0
    consts = (n2w, wg, wu, wd, fnw)
    return pl.pallas_call(
        _ffn_kernel,
        grid=(m // FFN_TILE,),
        in_specs=[pl.BlockSpec((FFN_TILE, D_MODEL), lambda i: (i, 0))]
        + [_const_spec(a.shape) for a in consts],
        out_specs=pl.BlockSpec((FFN_TILE, D_MODEL), lambda i: (i, 0)),
        out_shape=jax.ShapeDtypeStruct(h.shape, F32),
        scratch_shapes=[pltpu.VMEM((FFN_TILE, D_FF), BF16)],
        compiler_params=pltpu.CompilerParams(
            dimension_semantics=("parallel",),
            vmem_limit_bytes=VMEM_LIMIT),
        name="ffn",
    )(h, *consts)


def kernel(x, norm1_w, w_in, w_gate_up, b_gate_up, gla_norm_w, sg_ln_w, sg_ln_b, sg_w_s, sg_b_s,
           w_out, norm2_w, w_ffn_gate, w_ffn_up, w_ffn_down, final_norm_w):
    bsz, seq, _ = x.shape
    assert w_in.shape[0] == 1, "the final norm is fused into the (single) layer's ffn call"
    wi = w_in[0]
    o_glr = GLA_KEY + GLA_KEY + GLA_WIDTH + GATE_RANK
    wa = jnp.pad(wi[:, :o_glr], ((0, 0), (0, GLR_PAD - GATE_RANK))).astype(BF16)
    wb = wi[:, o_glr:].astype(BF16)
    wgu = jnp.concatenate(
        [w_gate_up[0], jnp.zeros((GLR_PAD - GATE_RANK, GLA_KEY), w_gate_up.dtype)],
        axis=0).astype(BF16)
    h, (wg, wu, wd) = _mixer(
        x, norm1_w[0][None, :], wa, wb, wgu, b_gate_up[0][None, :], gla_norm_w[0][None, :],
        sg_ln_w[0], sg_ln_b[0], sg_w_s[0].astype(BF16), jnp.transpose(sg_b_s[0]),
        w_out[0].astype(BF16), (w_ffn_gate[0], w_ffn_up[0], w_ffn_down[0]))
    out = _ffn(h.reshape(bsz * seq, D_MODEL), norm2_w[0][None, :], wg, wu, wd,
               final_norm_w[None, :])
    return out.reshape(bsz, seq, D_MODEL)
```

```python
import jax
import jax.numpy as jnp
from jax import lax
from jax.experimental import pallas as pl
from jax.experimental.pallas import tpu as pltpu

F32 = jnp.float32
BF16 = jnp.bfloat16

D_MODEL = 1024
GLA_HEADS = 4
GLA_DK = 64
GLA_DV = 128
GLA_KEY = GLA_HEADS * GLA_DK
GLA_WIDTH = GLA_HEADS * GLA_DV
GATE_RANK = 16
GATE_NORMALIZER = 16.0
GLA_CHUNK = 64
SG_GROUPS = 4
SG_CH = 128
SG_WIDTH = SG_GROUPS * SG_CH
SG_CHUNK = 128
D_FF = 2816
NORM_EPS = 1e-5

LANES = 128
BF16_SUBLANES = 16
GLR_PAD = LANES

C_Q = 0
C_K = C_Q + GLA_KEY
C_V = C_K + GLA_KEY
C_GLR = C_V + GLA_WIDTH
A_COLS = C_GLR + GLR_PAD
C_GOUT = A_COLS
C_SU = C_GOUT + GLA_WIDTH
C_SV = C_SU + SG_WIDTH

PAIR = 2 * GLA_CHUNK
HEADS_PER_GROUP = LANES // GLA_DK
PROJ_COLS = 256

MIX_TILE = 1024
MIX_SUB = 512
FFN_TILE = 1024
FFN_SUB = 512
FF_CHUNK = 256
DOWN_SPLIT = 2
FFN_LEAD = 1
VMEM_LIMIT = 56 * 1024 * 1024


def _rms(x, w):
    ms = jnp.mean(x * x, axis=-1, keepdims=True)
    return x * lax.rsqrt(ms + NORM_EPS) * w


def _gelu(x):
    return 0.5 * x * (1.0 + lax.erf(x * (0.5 ** 0.5)))


def _dot(a, b):
    return jnp.dot(a, b, preferred_element_type=F32)


def _dot_nt(a, b):
    return lax.dot_general(a, b, (((1,), (1,)), ((), ())), preferred_element_type=F32)


def _mixer_kernel(x_ref, n1w_ref, wa_ref, wb_ref, wgu_ref, bgu_ref, gnw_ref, lnw_ref, lnb_ref,
                  ws_ref, bst_ref, wout_ref, f0_ref, f1_ref, f2_ref,
                  o_ref, f0b_ref, f1b_ref, f2b_ref, st_ref, ogla_ref, mix_ref, wfold_ref):
    @pl.when(pl.program_id(1) == 0)
    def _():
        st_ref[...] = jnp.zeros_like(st_ref)

    @pl.when((pl.program_id(0) == 0) & (pl.program_id(1) == 0))
    def _():
        wfold_ref[...] = _dot(wa_ref[:, C_GLR:C_GLR + GLR_PAD], wgu_ref[...]).astype(BF16)

    st = st_ref[...]
    for s in range(x_ref.shape[1] // MIX_SUB):
        st = _mixer_sub_tile(s * MIX_SUB, st, x_ref, n1w_ref, wa_ref, wb_ref, wfold_ref, bgu_ref,
                             gnw_ref, lnw_ref, lnb_ref, ws_ref, bst_ref, wout_ref, o_ref, ogla_ref,
                             mix_ref)
    st_ref[...] = st

    for src, dst in ((f0_ref, f0b_ref), (f1_ref, f1b_ref), (f2_ref, f2b_ref)):
        dst[...] = src[...].astype(BF16)


def _mixer_sub_tile(base, st, x_ref, n1w_ref, wa_ref, wb_ref, wfold_ref, bgu_ref, gnw_ref, lnw_ref,
                    lnb_ref, ws_ref, bst_ref, wout_ref, o_ref, ogla_ref, mix_ref):
    tt = MIX_SUB
    n_pairs = tt // PAIR
    per = PROJ_COLS // LANES
    rows = slice(base, base + tt)

    x = x_ref[0, rows, :]
    n = _rms(x, n1w_ref[...]).astype(BF16)

    def proj(lo, width):
        if lo < A_COLS:
            return _dot(n, wa_ref[:, lo:lo + width])
        return _dot(n, wb_ref[:, lo - A_COLS:lo - A_COLS + width])

    def proj_wide(lo, j):
        return proj(lo + j * PROJ_COLS, PROJ_COLS)

    def cols(pieces, g):
        return pieces[g // per][:, (g % per) * LANES:(g % per + 1) * LANES]

    gk = _dot(n, wfold_ref[...]) + bgu_ref[...]
    sv = [proj_wide(C_SV, j) for j in range(SG_WIDTH // PROJ_COLS)]
    log_a = (jnp.minimum(gk, 0.0) - jnp.log1p(jnp.exp(-jnp.abs(gk)))) * (1.0 / GATE_NORMALIZER)
    k = proj(C_K, GLA_KEY)
    q = proj(C_Q, GLA_KEY)

    la_hi = log_a.astype(BF16)
    la_lo = (log_a - la_hi.astype(F32)).astype(BF16)
    r = lax.broadcasted_iota(jnp.int32, (PAIR, 2 * PAIR), 0)
    c = lax.broadcasted_iota(jnp.int32, (PAIR, 2 * PAIR), 1) % PAIR
    tri2 = (r >= c).astype(BF16)
    b_parts = []
    for p in range(n_pairs):
        psl = slice(p * PAIR, (p + 1) * PAIR)
        b_parts.append(_dot(tri2, jnp.concatenate([la_hi[psl], la_lo[psl]], axis=0)))
    v = proj(C_V, GLA_WIDTH)

    y = []
    for g in range(SG_GROUPS):
        v_g = _gelu(cols(sv, g))
        mu = jnp.mean(v_g, axis=-1, keepdims=True)
        d = v_g - mu
        var = jnp.mean(d * d, axis=-1, keepdims=True)
        y.append((d * lax.rsqrt(var + NORM_EPS) * lnw_ref[g:g + 1, :]
                  + lnb_ref[g:g + 1, :]).astype(BF16))

    lane_head = lax.broadcasted_iota(jnp.int32, (PAIR, GLA_KEY), 1) // GLA_DK
    pr = lax.broadcasted_iota(jnp.int32, (GLA_HEADS * PAIR, PAIR), 0) % PAIR
    pc = lax.broadcasted_iota(jnp.int32, (GLA_HEADS * PAIR, PAIR), 1)
    causal = pr >= pc
    def gla_scores(p):
        psl = slice(p * PAIR, (p + 1) * PAIR)
        b = b_parts[p]
        b_mid = b[GLA_CHUNK - 1:GLA_CHUNK]
        b_end = b[PAIR - 1:PAIR]
        qs = q[psl] * (GLA_DK ** -0.5)
        kp = k[psl]
        q_rel = qs * jnp.exp(b - b_mid)
        k_rel = (kp * jnp.exp(b_mid - b)).astype(BF16)
        q_in = qs * jnp.exp(b)
        k_out_t = (kp * jnp.exp(b_end - b)).T.astype(BF16)
        decay_t = jnp.exp(jnp.broadcast_to(b_end, (LANES, GLA_KEY)).T)
        q_stack = jnp.concatenate(
            [jnp.where(lane_head == h, q_rel, 0.0) for h in range(GLA_HEADS)], axis=0).astype(BF16)
        probs = jnp.where(causal, _dot_nt(q_stack, k_rel), 0.0).astype(BF16)
        q_heads = [jnp.where(lane_head == h, q_in, 0.0)[:, (h // HEADS_PER_GROUP) * LANES:
                                                         (h // HEADS_PER_GROUP + 1) * LANES].astype(BF16)
                   for h in range(GLA_HEADS)]
        return probs, q_heads, k_out_t, decay_t

    def gla_apply(p, st, probs, q_heads, k_out_t, decay_t):
        vp_b = v[p * PAIR:(p + 1) * PAIR].astype(BF16)
        st_b = st.astype(BF16)
        o, u = [], []
        for h in range(GLA_HEADS):
            g = h // HEADS_PER_GROUP
            v_h = vp_b[:, h * GLA_DV:(h + 1) * GLA_DV]
            lhs = jnp.concatenate([probs[h * PAIR:(h + 1) * PAIR], q_heads[h]], axis=1)
            rhs = jnp.concatenate([v_h, st_b[g * LANES:(g + 1) * LANES]], axis=0)
            o.append(_dot(lhs, rhs))
            u.append(_dot(k_out_t[h * GLA_DK:(h + 1) * GLA_DK], v_h))
        ogla_ref[base + p * PAIR:base + (p + 1) * PAIR, :] = jnp.concatenate(o, axis=1)
        return st * decay_t + jnp.concatenate(u, axis=0)

    wr = lax.broadcasted_iota(jnp.int32, (SG_CHUNK, SG_CHUNK), 0)
    wc = lax.broadcasted_iota(jnp.int32, (SG_CHUNK, SG_CHUNK), 1)

    def sg_mix(su):
        for g in range(SG_GROUPS):
            u_g = _gelu(cols(su, g))
            w_g = jnp.where(wr >= wc, ws_ref[g], jnp.zeros((), BF16))
            bias = bst_ref[:, g:g + 1]
            for p in range(tt // SG_CHUNK):
                rs = slice(p * SG_CHUNK, (p + 1) * SG_CHUNK)
                mixed = _dot(w_g, y[g][rs]) + bias
                mix_ref[base + p * SG_CHUNK:base + (p + 1) * SG_CHUNK,
                        GLA_WIDTH + g * SG_CH:GLA_WIDTH + (g + 1) * SG_CH] = (
                            u_g[rs] * mixed).astype(BF16)

    su = [proj_wide(C_SU, j) for j in range(SG_WIDTH // PROJ_COLS)]
    go = []
    acc = []
    fills = [lambda: sg_mix(su),
             lambda: go.append(proj_wide(C_GOUT, 0)),
             lambda: go.append(proj_wide(C_GOUT, 1)),
             lambda: acc.append(x + _dot(mix_ref[rows, GLA_WIDTH:], wout_ref[GLA_WIDTH:, :]))]
    assert len(fills) == n_pairs
    staged = gla_scores(0)
    for p in range(n_pairs):
        fills[p]()
        nxt = gla_scores(p + 1) if p + 1 < n_pairs else None
        st = gla_apply(p, st, *staged)
        staged = nxt

    acc = acc[0]
    for h in range(GLA_HEADS):
        hs = slice(h * GLA_DV, (h + 1) * GLA_DV)
        o_h = _rms(ogla_ref[rows, hs], gnw_ref[...])
        g_h = cols(go, h)
        mix_ref[rows, hs] = (o_h * (g_h * jax.nn.sigmoid(g_h))).astype(BF16)
    o_ref[0, rows, :] = acc + _dot(mix_ref[rows, :GLA_WIDTH], wout_ref[:GLA_WIDTH, :])
    return st


def _ffn_kernel(h_ref, n2w_ref, wg_ref, wu_ref, wd_ref, fnw_ref, o_ref, hid_ref):
    n_sub = h_ref.shape[0] // FFN_SUB
    n_chunks = D_FF // FF_CHUNK
    subs = [slice(s * FFN_SUB, (s + 1) * FFN_SUB) for s in range(n_sub)]
    n2 = [None] * n_sub

    def gate_up(s, f):
        if n2[s] is None:
            n2[s] = _rms(h_ref[subs[s], :], n2w_ref[...]).astype(BF16)
        fs = slice(f * FF_CHUNK, (f + 1) * FF_CHUNK)
        a = _dot(n2[s], wg_ref[:, fs])
        u = _dot(n2[s], wu_ref[:, fs])
        hid_ref[subs[s], fs] = (a * jax.nn.sigmoid(a) * u).astype(BF16)

    def down(s):
        piece = FFN_SUB // DOWN_SPLIT
        for j in range(DOWN_SPLIT):
            rows = slice(subs[s].start + j * piece, subs[s].start + (j + 1) * piece)
            h2 = h_ref[rows, :] + _dot(hid_ref[rows, :], wd_ref[...])
            o_ref[rows, :] = _rms(h2, fnw_ref[...])

    for s in range(n_sub):
        for f in range(FFN_LEAD if s else 0, n_chunks):
            gate_up(s, f)
        if s + 1 < n_sub:
            for f in range(FFN_LEAD):
                gate_up(s + 1, f)
        down(s)


def _const_spec(shape):
    nd = len(shape)
    return pl.BlockSpec(shape, lambda *_: (0,) * nd, pipeline_mode=pl.Buffered(1))


def _mixer(x, n1w, wa, wb, wgu, bgu, gnw, lnw, lnb, ws, bst, wout, ffn_ws):
    bsz, seq, _ = x.shape
    assert seq % MIX_TILE == 0 and MIX_TILE % MIX_SUB == 0 and MIX_SUB % PAIR == 0
    n_t = seq // MIX_TILE
    consts = (n1w, wa, wb, wgu, bgu, gnw, lnw, lnb, ws, bst, wout)

    def slab_spec(w):
        rows, width = w.shape
        slab = next(r for r in range(BF16_SUBLANES, rows + 1, BF16_SUBLANES)
                    if rows % r == 0 and r * bsz * n_t >= rows)
        last = rows // slab - 1
        return pl.BlockSpec((slab, width), lambda b, t: (jnp.minimum(b * n_t + t, last), 0))

    slab_specs = [slab_spec(w) for w in ffn_ws]
    tile_spec = pl.BlockSpec((1, MIX_TILE, D_MODEL), lambda b, t: (b, t, 0))
    h, *narrowed = pl.pallas_call(
        _mixer_kernel,
        grid=(bsz, n_t),
        in_specs=[tile_spec] + [_const_spec(a.shape) for a in consts] + slab_specs,
        out_specs=[tile_spec] + slab_specs,
        out_shape=[jax.ShapeDtypeStruct(x.shape, F32)]
        + [jax.ShapeDtypeStruct(w.shape, BF16) for w in ffn_ws],
        scratch_shapes=[
            pltpu.VMEM((GLA_KEY, GLA_DV), F32),
            pltpu.VMEM((MIX_TILE, GLA_WIDTH), F32),
            pltpu.VMEM((MIX_TILE, D_MODEL), BF16),
            pltpu.VMEM((D_MODEL, GLA_KEY), BF16),
        ],
        compiler_params=pltpu.CompilerParams(
            dimension_semantics=("arbitrary", "arbitrary"),
            vmem_limit_bytes=VMEM_LIMIT),
        name="mixer",
    )(x, *consts, *ffn_ws)
    return h, narrowed


def _ffn(h, n2w, wg, wu, wd, fnw):
    m = h.shape[0]
    assert m % FFN_TILE == 0
    consts = (n2w, wg, wu, wd, fnw)
    return pl.pallas_call(
        _ffn_kernel,
        grid=(m // FFN_TILE,),
        in_specs=[pl.BlockSpec((FFN_TILE, D_MODEL), lambda i: (i, 0))]
        + [_const_spec(a.shape) for a in consts],
        out_specs=pl.BlockSpec((FFN_TILE, D_MODEL), lambda i: (i, 0)),
        out_shape=jax.ShapeDtypeStruct(h.shape, F32),
        scratch_shapes=[pltpu.VMEM((FFN_TILE, D_FF), BF16)],
        compiler_params=pltpu.CompilerParams(
            dimension_semantics=("parallel",),
            vmem_limit_bytes=VMEM_LIMIT),
        name="ffn",
    )(h, *consts)


def kernel(x, norm1_w, w_in, w_gate_up, b_gate_up, gla_norm_w, sg_ln_w, sg_ln_b, sg_w_s, sg_b_s,
           w_out, norm2_w, w_ffn_gate, w_ffn_up, w_ffn_down, final_norm_w):
    bsz, seq, _ = x.shape
    assert w_in.shape[0] == 1, "the final norm is fused into the (single) layer's ffn call"
    wi = w_in[0]
    o_glr = GLA_KEY + GLA_KEY + GLA_WIDTH + GATE_RANK
    wa = jnp.pad(wi[:, :o_glr], ((0, 0), (0, GLR_PAD - GATE_RANK))).astype(BF16)
    wb = wi[:, o_glr:].astype(BF16)
    wgu = jnp.concatenate(
        [w_gate_up[0], jnp.zeros((GLR_PAD - GATE_RANK, GLA_KEY), w_gate_up.dtype)],
        axis=0).astype(BF16)
    h, (wg, wu, wd) = _mixer(
        x, norm1_w[0][None, :], wa, wb, wgu, b_gate_up[0][None, :], gla_norm_w[0][None, :],
        sg_ln_w[0], sg_ln_b[0], sg_w_s[0].astype(BF16), jnp.transpose(sg_b_s[0]),
        w_out[0].astype(BF16), (w_ffn_gate[0], w_ffn_up[0], w_ffn_down[0]))
    out = _ffn(h.reshape(bsz * seq, D_MODEL), norm2_w[0][None, :], wg, wu, wd,
               final_norm_w[None, :])
    return out.reshape(bsz, seq, D_MODEL)
```

```python
import jax
import jax.numpy as jnp
from jax import lax
from jax.experimental import pallas as pl
from jax.experimental.pallas import tpu as pltpu

F32 = jnp.float32
BF16 = jnp.bfloat16

D_MODEL = 1024
GLA_HEADS = 4
GLA_DK = 64
GLA_DV = 128
GLA_KEY = GLA_HEADS * GLA_DK
GLA_WIDTH = GLA_HEADS * GLA_DV
GATE_RANK = 16
GATE_NORMALIZER = 16.0
GLA_CHUNK = 64
SG_GROUPS = 4
SG_CH = 128
SG_WIDTH = SG_GROUPS * SG_CH
SG_CHUNK = 128
D_FF = 2816
NORM_EPS = 1e-5

LANES = 128
BF16_SUBLANES = 16
GLR_PAD = LANES

C_Q = 0
C_K = C_Q + GLA_KEY
C_V = C_K + GLA_KEY
C_GLR = C_V + GLA_WIDTH
A_COLS = C_GLR + GLR_PAD
C_GOUT = A_COLS
C_SU = C_GOUT + GLA_WIDTH
C_SV = C_SU + SG_WIDTH

PAIR = 2 * GLA_CHUNK
HEADS_PER_GROUP = LANES // GLA_DK
PROJ_COLS = 256

MIX_TILE = 1024
MIX_SUB = 512
FFN_TILE = 1024
FFN_SUB = 512
FF_CHUNK = 256
DOWN_SPLIT = 2
FFN_LEAD = 1
VMEM_LIMIT = 56 * 1024 * 1024


def _rms(x, w):
    ms = jnp.mean(x * x, axis=-1, keepdims=True)
    return x * lax.rsqrt(ms + NORM_EPS) * w


def _gelu(x):
    return 0.5 * x * (1.0 + lax.erf(x * (0.5 ** 0.5)))


def _dot(a, b):
    return jnp.dot(a, b, preferred_element_type=F32)


def _dot_nt(a, b):
    return lax.dot_general(a, b, (((1,), (1,)), ((), ())), preferred_element_type=F32)


def _mixer_kernel(x_ref, n1w_ref, wa_ref, wb_ref, wgu_ref, bgu_ref, gnw_ref, lnw_ref, lnb_ref,
                  ws_ref, bst_ref, wout_ref, f0_ref, f1_ref, f2_ref,
                  o_ref, f0b_ref, f1b_ref, f2b_ref, st_ref, ogla_ref, mix_ref, wfold_ref):
    @pl.when(pl.program_id(1) == 0)
    def _():
        st_ref[...] = jnp.zeros_like(st_ref)

    @pl.when((pl.program_id(0) == 0) & (pl.program_id(1) == 0))
    def _():
        wfold_ref[...] = _dot(wa_ref[:, C_GLR:C_GLR + GLR_PAD], wgu_ref[...]).astype(BF16)

    st = st_ref[...]
    for s in range(x_ref.shape[1] // MIX_SUB):
        st = _mixer_sub_tile(s * MIX_SUB, st, x_ref, n1w_ref, wa_ref, wb_ref, wfold_ref, bgu_ref,
                             gnw_ref, lnw_ref, lnb_ref, ws_ref, bst_ref, wout_ref, o_ref, ogla_ref,
                             mix_ref)
    st_ref[...] = st

    for src, dst in ((f0_ref, f0b_ref), (f1_ref, f1b_ref), (f2_ref, f2b_ref)):
        dst[...] = src[...].astype(BF16)


def _mixer_sub_tile(base, st, x_ref, n1w_ref, wa_ref, wb_ref, wfold_ref, bgu_ref, gnw_ref, lnw_ref,
                    lnb_ref, ws_ref, bst_ref, wout_ref, o_ref, ogla_ref, mix_ref):
    tt = MIX_SUB
    n_pairs = tt // PAIR
    per = PROJ_COLS // LANES
    rows = slice(base, base + tt)

    x = x_ref[0, rows, :]
    n = _rms(x, n1w_ref[...]).astype(BF16)

    def proj(lo, width):
        if lo < A_COLS:
            return _dot(n, wa_ref[:, lo:lo + width])
        return _dot(n, wb_ref[:, lo - A_COLS:lo - A_COLS + width])

    def proj_wide(lo, j):
        return proj(lo + j * PROJ_COLS, PROJ_COLS)

    def cols(pieces, g):
        return pieces[g // per][:, (g % per) * LANES:(g % per + 1) * LANES]

    gk = _dot(n, wfold_ref[...]) + bgu_ref[...]
    sv = [proj_wide(C_SV, j) for j in range(SG_WIDTH // PROJ_COLS)]
    log_a = (jnp.minimum(gk, 0.0) - jnp.log1p(jnp.exp(-jnp.abs(gk)))) * (1.0 / GATE_NORMALIZER)
    k = proj(C_K, GLA_KEY)
    q = proj(C_Q, GLA_KEY)

    la_hi = log_a.astype(BF16)
    la_lo = (log_a - la_hi.astype(F32)).astype(BF16)
    r = lax.broadcasted_iota(jnp.int32, (PAIR, 2 * PAIR), 0)
    c = lax.broadcasted_iota(jnp.int32, (PAIR, 2 * PAIR), 1) % PAIR
    tri2 = (r >= c).astype(BF16)
    b_parts = []
    for p in range(n_pairs):
        psl = slice(p * PAIR, (p + 1) * PAIR)
        b_parts.append(_dot(tri2, jnp.concatenate([la_hi[psl], la_lo[psl]], axis=0)))
    v = proj(C_V, GLA_WIDTH)

    y = []
    for g in range(SG_GROUPS):
        v_g = _gelu(cols(sv, g))
        mu = jnp.mean(v_g, axis=-1, keepdims=True)
        d = v_g - mu
        var = jnp.mean(d * d, axis=-1, keepdims=True)
        y.append((d * lax.rsqrt(var + NORM_EPS) * lnw_ref[g:g + 1, :]
                  + lnb_ref[g:g + 1, :]).astype(BF16))

    lane_head = lax.broadcasted_iota(jnp.int32, (PAIR, GLA_KEY), 1) // GLA_DK
    pr = lax.broadcasted_iota(jnp.int32, (GLA_HEADS * PAIR, PAIR), 0) % PAIR
    pc = lax.broadcasted_iota(jnp.int32, (GLA_HEADS * PAIR, PAIR), 1)
    causal = pr >= pc
    def gla_scores(p):
        psl = slice(p * PAIR, (p + 1) * PAIR)
        b = b_parts[p]
        b_mid = b[GLA_CHUNK - 1:GLA_CHUNK]
        b_end = b[PAIR - 1:PAIR]
        qs = q[psl] * (GLA_DK ** -0.5)
        kp = k[psl]
        q_rel = qs * jnp.exp(b - b_mid)
        k_rel = (kp * jnp.exp(b_mid - b)).astype(BF16)
        q_in = qs * jnp.exp(b)
        k_out_t = (kp * jnp.exp(b_end - b)).T.astype(BF16)
        decay_t = jnp.exp(jnp.broadcast_to(b_end, (LANES, GLA_KEY)).T)
        scores = []
        for g in range(GLA_HEADS // HEADS_PER_GROUP):
            lanes = slice(g * LANES, (g + 1) * LANES)
            q_stack = jnp.concatenate(
                [jnp.where(lane_head == h, q_rel, 0.0)[:, lanes]
                 for h in range(g * HEADS_PER_GROUP, (g + 1) * HEADS_PER_GROUP)], axis=0)
            scores.append(_dot_nt(q_stack.astype(BF16), k_rel[:, lanes]))
        probs = jnp.where(causal, jnp.concatenate(scores, axis=0), 0.0).astype(BF16)
        q_heads = [jnp.where(lane_head == h, q_in, 0.0)[:, (h // HEADS_PER_GROUP) * LANES:
                                                         (h // HEADS_PER_GROUP + 1) * LANES].astype(BF16)
                   for h in range(GLA_HEADS)]
        return probs, q_heads, k_out_t, decay_t

    def gla_apply(p, st, probs, q_heads, k_out_t, decay_t):
        vp_b = v[p * PAIR:(p + 1) * PAIR].astype(BF16)
        st_b = st.astype(BF16)
        o, u = [], []
        for h in range(GLA_HEADS):
            g = h // HEADS_PER_GROUP
            v_h = vp_b[:, h * GLA_DV:(h + 1) * GLA_DV]
            lhs = jnp.concatenate([probs[h * PAIR:(h + 1) * PAIR], q_heads[h]], axis=1)
            rhs = jnp.concatenate([v_h, st_b[g * LANES:(g + 1) * LANES]], axis=0)
            o.append(_dot(lhs, rhs))
            u.append(_dot(k_out_t[h * GLA_DK:(h + 1) * GLA_DK], v_h))
        ogla_ref[base + p * PAIR:base + (p + 1) * PAIR, :] = jnp.concatenate(o, axis=1)
        return st * decay_t + jnp.concatenate(u, axis=0)

    wr = lax.broadcasted_iota(jnp.int32, (SG_CHUNK, SG_CHUNK), 0)
    wc = lax.broadcasted_iota(jnp.int32, (SG_CHUNK, SG_CHUNK), 1)

    def sg_mix(su):
        for g in range(SG_GROUPS):
            u_g = _gelu(cols(su, g))
            w_g = jnp.where(wr >= wc, ws_ref[g], jnp.zeros((), BF16))
            bias = bst_ref[:, g:g + 1]
            for p in range(tt // SG_CHUNK):
                rs = slice(p * SG_CHUNK, (p + 1) * SG_CHUNK)
                mixed = _dot(w_g, y[g][rs]) + bias
                mix_ref[base + p * SG_CHUNK:base + (p + 1) * SG_CHUNK,
                        GLA_WIDTH + g * SG_CH:GLA_WIDTH + (g + 1) * SG_CH] = (
                            u_g[rs] * mixed).astype(BF16)

    su = [proj_wide(C_SU, j) for j in range(SG_WIDTH // PROJ_COLS)]
    go = []
    acc = []
    fills = [lambda: sg_mix(su),
             lambda: go.append(proj_wide(C_GOUT, 0)),
             lambda: go.append(proj_wide(C_GOUT, 1)),
             lambda: acc.append(x + _dot(mix_ref[rows, GLA_WIDTH:], wout_ref[GLA_WIDTH:, :]))]
    assert len(fills) == n_pairs
    staged = gla_scores(0)
    for p in range(n_pairs):
        fills[p]()
        nxt = gla_scores(p + 1) if p + 1 < n_pairs else None
        st = gla_apply(p, st, *staged)
        staged = nxt

    acc = acc[0]
    for h in range(GLA_HEADS):
        hs = slice(h * GLA_DV, (h + 1) * GLA_DV)
        o_h = _rms(ogla_ref[rows, hs], gnw_ref[...])
        g_h = cols(go, h)
        mix_ref[rows, hs] = (o_h * (g_h * jax.nn.sigmoid(g_h))).astype(BF16)
    o_ref[0, rows, :] = acc + _dot(mix_ref[rows, :GLA_WIDTH], wout_ref[:GLA_WIDTH, :])
    return st


def _ffn_kernel(h_ref, n2w_ref, wg_ref, wu_ref, wd_ref, fnw_ref, o_ref, hid_ref):
    n_sub = h_ref.shape[0] // FFN_SUB
    n_chunks = D_FF // FF_CHUNK
    subs = [slice(s * FFN_SUB, (s + 1) * FFN_SUB) for s in range(n_sub)]
    n2 = [None] * n_sub

    def gate_up(s, f):
        if n2[s] is None:
            n2[s] = _rms(h_ref[subs[s], :], n2w_ref[...]).astype(BF16)
        fs = slice(f * FF_CHUNK, (f + 1) * FF_CHUNK)
        a = _dot(n2[s], wg_ref[:, fs])
        u = _dot(n2[s], wu_ref[:, fs])
        hid_ref[subs[s], fs] = (a * jax.nn.sigmoid(a) * u).astype(BF16)

    def down(s):
        piece = FFN_SUB // DOWN_SPLIT
        for j in range(DOWN_SPLIT):
            rows = slice(subs[s].start + j * piece, subs[s].start + (j + 1) * piece)
            h2 = h_ref[rows, :] + _dot(hid_ref[rows, :], wd_ref[...])
            o_ref[rows, :] = _rms(h2, fnw_ref[...])

    for s in range(n_sub):
        for f in range(FFN_LEAD if s else 0, n_chunks):
            gate_up(s, f)
        if s + 1 < n_sub:
            for f in range(FFN_LEAD):
                gate_up(s + 1, f)
        down(s)


def _const_spec(shape):
    nd = len(shape)
    return pl.BlockSpec(shape, lambda *_: (0,) * nd, pipeline_mode=pl.Buffered(1))


def _mixer(x, n1w, wa, wb, wgu, bgu, gnw, lnw, lnb, ws, bst, wout, ffn_ws):
    bsz, seq, _ = x.shape
    assert seq % MIX_TILE == 0 and MIX_TILE % MIX_SUB == 0 and MIX_SUB % PAIR == 0
    n_t = seq // MIX_TILE
    consts = (n1w, wa, wb, wgu, bgu, gnw, lnw, lnb, ws, bst, wout)

    def slab_spec(w):
        rows, width = w.shape
        slab = next(r for r in range(BF16_SUBLANES, rows + 1, BF16_SUBLANES)
                    if rows % r == 0 and r * bsz * n_t >= rows)
        last = rows // slab - 1
        return pl.BlockSpec((slab, width), lambda b, t: (jnp.minimum(b * n_t + t, last), 0))

    slab_specs = [slab_spec(w) for w in ffn_ws]
    tile_spec = pl.BlockSpec((1, MIX_TILE, D_MODEL), lambda b, t: (b, t, 0))
    h, *narrowed = pl.pallas_call(
        _mixer_kernel,
        grid=(bsz, n_t),
        in_specs=[tile_spec] + [_const_spec(a.shape) for a in consts] + slab_specs,
        out_specs=[tile_spec] + slab_specs,
        out_shape=[jax.ShapeDtypeStruct(x.shape, F32)]
        + [jax.ShapeDtypeStruct(w.shape, BF16) for w in ffn_ws],
        scratch_shapes=[
            pltpu.VMEM((GLA_KEY, GLA_DV), F32),
            pltpu.VMEM((MIX_TILE, GLA_WIDTH), F32),
            pltpu.VMEM((MIX_TILE, D_MODEL), BF16),
            pltpu.VMEM((D_MODEL, GLA_KEY), BF16),
        ],
        compiler_params=pltpu.CompilerParams(
            dimension_semantics=("arbitrary", "arbitrary"),
            vmem_limit_bytes=VMEM_LIMIT),
        name="mixer",
    )(x, *consts, *ffn_ws)
    return h, narrowed


def _ffn(h, n2w, wg, wu, wd, fnw):
    m = h.shape[0]
    assert m % FFN_TILE == 0
    consts = (n2w, wg, wu, wd, fnw)
    return pl.pallas_call(
        _ffn_kernel,
        grid=(m // FFN_TILE,),
        in_specs=[pl.BlockSpec((FFN_TILE, D_MODEL), lambda i: (i, 0))]
        + [_const_spec(a.shape) for a in consts],
        out_specs=pl.BlockSpec((FFN_TILE, D_MODEL), lambda i: (i, 0)),
        out_shape=jax.ShapeDtypeStruct(h.shape, F32),
        scratch_shapes=[pltpu.VMEM((FFN_TILE, D_FF), BF16)],
        compiler_params=pltpu.CompilerParams(
            dimension_semantics=("parallel",),
            vmem_limit_bytes=VMEM_LIMIT),
        name="ffn",
    )(h, *consts)


def kernel(x, norm1_w, w_in, w_gate_up, b_gate_up, gla_norm_w, sg_ln_w, sg_ln_b, sg_w_s, sg_b_s,
           w_out, norm2_w, w_ffn_gate, w_ffn_up, w_ffn_down, final_norm_w):
    bsz, seq, _ = x.shape
    assert w_in.shape[0] == 1, "the final norm is fused into the (single) layer's ffn call"
    wi = w_in[0]
    o_glr = GLA_KEY + GLA_KEY + GLA_WIDTH + GATE_RANK
    wa = jnp.pad(wi[:, :o_glr], ((0, 0), (0, GLR_PAD - GATE_RANK))).astype(BF16)
    wb = wi[:, o_glr:].astype(BF16)
    wgu = jnp.concatenate(
        [w_gate_up[0], jnp.zeros((GLR_PAD - GATE_RANK, GLA_KEY), w_gate_up.dtype)],
        axis=0).astype(BF16)
    h, (wg, wu, wd) = _mixer(
        x, norm1_w[0][None, :], wa, wb, wgu, b_gate_up[0][None, :], gla_norm_w[0][None, :],
        sg_ln_w[0], sg_ln_b[0], sg_w_s[0].astype(BF16), jnp.transpose(sg_b_s[0]),
        w_out[0].astype(BF16), (w_ffn_gate[0], w_ffn_up[0], w_ffn_down[0]))
    out = _ffn(h.reshape(bsz * seq, D_MODEL), norm2_w[0][None, :], wg, wu, wd,
               final_norm_w[None, :])
    return out.reshape(bsz, seq, D_MODEL)
```

```python
import jax
import jax.numpy as jnp
from jax import lax
from jax.experimental import pallas as pl
from jax.experimental.pallas import tpu as pltpu

F32 = jnp.float32
BF16 = jnp.bfloat16

D_MODEL = 1024
GLA_HEADS = 4
GLA_DK = 64
GLA_DV = 128
GLA_KEY = GLA_HEADS * GLA_DK
GLA_WIDTH = GLA_HEADS * GLA_DV
GATE_RANK = 16
GATE_NORMALIZER = 16.0
GLA_CHUNK = 64
SG_GROUPS = 4
SG_CH = 128
SG_WIDTH = SG_GROUPS * SG_CH
SG_CHUNK = 128
D_FF = 2816
NORM_EPS = 1e-5

LANES = 128
BF16_SUBLANES = 16
GLR_PAD = LANES

C_Q = 0
C_K = C_Q + GLA_KEY
C_V = C_K + GLA_KEY
C_GLR = C_V + GLA_WIDTH
A_COLS = C_GLR + GLR_PAD
C_GOUT = A_COLS
C_SU = C_GOUT + GLA_WIDTH
C_SV = C_SU + SG_WIDTH

PAIR = 2 * GLA_CHUNK
HEADS_PER_GROUP = LANES // GLA_DK
PROJ_COLS = 256

MIX_TILE = 2048
MIX_SUB = 512
FFN_TILE = 1024
FFN_SUB = 512
FF_CHUNK = 256
DOWN_SPLIT = 2
FFN_LEAD = 1
VMEM_LIMIT = 56 * 1024 * 1024


def _rms(x, w):
    ms = jnp.mean(x * x, axis=-1, keepdims=True)
    return x * lax.rsqrt(ms + NORM_EPS) * w


def _gelu(x):
    return 0.5 * x * (1.0 + lax.erf(x * (0.5 ** 0.5)))


def _dot(a, b):
    return jnp.dot(a, b, preferred_element_type=F32)


def _dot_nt(a, b):
    return lax.dot_general(a, b, (((1,), (1,)), ((), ())), preferred_element_type=F32)


def _mixer_kernel(x_ref, n1w_ref, wa_ref, wb_ref, wgu_ref, bgu_ref, gnw_ref, lnw_ref, lnb_ref,
                  ws_ref, bst_ref, wout_ref, f0_ref, f1_ref, f2_ref,
                  o_ref, f0b_ref, f1b_ref, f2b_ref, st_ref, ogla_ref, mix_ref, wfold_ref):
    @pl.when(pl.program_id(1) == 0)
    def _():
        st_ref[...] = jnp.zeros_like(st_ref)

    @pl.when((pl.program_id(0) == 0) & (pl.program_id(1) == 0))
    def _():
        wfold_ref[...] = _dot(wa_ref[:, C_GLR:C_GLR + GLR_PAD], wgu_ref[...]).astype(BF16)

    st = st_ref[...]
    for s in range(x_ref.shape[1] // MIX_SUB):
        st = _mixer_sub_tile(s * MIX_SUB, st, x_ref, n1w_ref, wa_ref, wb_ref, wfold_ref, bgu_ref,
                             gnw_ref, lnw_ref, lnb_ref, ws_ref, bst_ref, wout_ref, o_ref, ogla_ref,
                             mix_ref)
    st_ref[...] = st

    for src, dst in ((f0_ref, f0b_ref), (f1_ref, f1b_ref), (f2_ref, f2b_ref)):
        dst[...] = src[...].astype(BF16)


def _mixer_sub_tile(base, st, x_ref, n1w_ref, wa_ref, wb_ref, wfold_ref, bgu_ref, gnw_ref, lnw_ref,
                    lnb_ref, ws_ref, bst_ref, wout_ref, o_ref, ogla_ref, mix_ref):
    tt = MIX_SUB
    n_pairs = tt // PAIR
    per = PROJ_COLS // LANES
    rows = slice(base, base + tt)

    x = x_ref[0, rows, :]
    n = _rms(x, n1w_ref[...]).astype(BF16)

    def proj(lo, width):
        if lo < A_COLS:
            return _dot(n, wa_ref[:, lo:lo + width])
        return _dot(n, wb_ref[:, lo - A_COLS:lo - A_COLS + width])

    def proj_wide(lo, j):
        return proj(lo + j * PROJ_COLS, PROJ_COLS)

    def cols(pieces, g):
        return pieces[g // per][:, (g % per) * LANES:(g % per + 1) * LANES]

    gk = _dot(n, wfold_ref[...]) + bgu_ref[...]
    sv = [proj_wide(C_SV, j) for j in range(SG_WIDTH // PROJ_COLS)]
    log_a = (jnp.minimum(gk, 0.0) - jnp.log1p(jnp.exp(-jnp.abs(gk)))) * (1.0 / GATE_NORMALIZER)
    k = proj(C_K, GLA_KEY)
    q = proj(C_Q, GLA_KEY)

    la_hi = log_a.astype(BF16)
    la_lo = (log_a - la_hi.astype(F32)).astype(BF16)
    r = lax.broadcasted_iota(jnp.int32, (PAIR, 2 * PAIR), 0)
    c = lax.broadcasted_iota(jnp.int32, (PAIR, 2 * PAIR), 1) % PAIR
    tri2 = (r >= c).astype(BF16)
    b_parts = []
    for p in range(n_pairs):
        psl = slice(p * PAIR, (p + 1) * PAIR)
        b_parts.append(_dot(tri2, jnp.concatenate([la_hi[psl], la_lo[psl]], axis=0)))
    v = proj(C_V, GLA_WIDTH)

    y = []
    for g in range(SG_GROUPS):
        v_g = _gelu(cols(sv, g))
        mu = jnp.mean(v_g, axis=-1, keepdims=True)
        d = v_g - mu
        var = jnp.mean(d * d, axis=-1, keepdims=True)
        y.append((d * lax.rsqrt(var + NORM_EPS) * lnw_ref[g:g + 1, :]
                  + lnb_ref[g:g + 1, :]).astype(BF16))

    lane_head = lax.broadcasted_iota(jnp.int32, (PAIR, GLA_KEY), 1) // GLA_DK
    pr = lax.broadcasted_iota(jnp.int32, (GLA_HEADS * PAIR, PAIR), 0) % PAIR
    pc = lax.broadcasted_iota(jnp.int32, (GLA_HEADS * PAIR, PAIR), 1)
    causal = pr >= pc

    def gla_scores(p):
        psl = slice(p * PAIR, (p + 1) * PAIR)
        b = b_parts[p]
        b_mid = b[GLA_CHUNK - 1:GLA_CHUNK]
        b_end = b[PAIR - 1:PAIR]
        qs = q[psl] * (GLA_DK ** -0.5)
        kp = k[psl]
        q_rel = qs * jnp.exp(b - b_mid)
        k_rel = (kp * jnp.exp(b_mid - b)).astype(BF16)
        q_in = qs * jnp.exp(b)
        k_out_t = (kp * jnp.exp(b_end - b)).T.astype(BF16)
        decay_t = jnp.exp(jnp.broadcast_to(b_end, (LANES, GLA_KEY)).T)
        scores = []
        for g in range(GLA_HEADS // HEADS_PER_GROUP):
            lanes = slice(g * LANES, (g + 1) * LANES)
            q_stack = jnp.concatenate(
                [jnp.where(lane_head == h, q_rel, 0.0)[:, lanes]
                 for h in range(g * HEADS_PER_GROUP, (g + 1) * HEADS_PER_GROUP)], axis=0)
            scores.append(_dot_nt(q_stack.astype(BF16), k_rel[:, lanes]))
        probs = jnp.where(causal, jnp.concatenate(scores, axis=0), 0.0).astype(BF16)
        q_heads = [jnp.where(lane_head == h, q_in, 0.0)[:, (h // HEADS_PER_GROUP) * LANES:
                                                         (h // HEADS_PER_GROUP + 1) * LANES].astype(BF16)
                   for h in range(GLA_HEADS)]
        return probs, q_heads, k_out_t, decay_t

    def gla_apply(p, st, probs, q_heads, k_out_t, decay_t):
        vp_b = v[p * PAIR:(p + 1) * PAIR].astype(BF16)
        st_b = st.astype(BF16)
        zeros = jnp.zeros((GLA_DK, LANES), BF16)
        o, u = [], []
        for h in range(GLA_HEADS):
            g = h // HEADS_PER_GROUP
            v_h = vp_b[:, h * GLA_DV:(h + 1) * GLA_DV]
            lhs = jnp.concatenate(
                [jnp.concatenate([probs[h * PAIR:(h + 1) * PAIR], q_heads[h]], axis=1),
                 jnp.concatenate([k_out_t[h * GLA_DK:(h + 1) * GLA_DK], zeros], axis=1)], axis=0)
            rhs = jnp.concatenate([v_h, st_b[g * LANES:(g + 1) * LANES]], axis=0)
            ou = _dot(lhs, rhs)
            o.append(ou[:PAIR])
            u.append(ou[PAIR:])
        ogla_ref[base + p * PAIR:base + (p + 1) * PAIR, :] = jnp.concatenate(o, axis=1)
        return st * decay_t + jnp.concatenate(u, axis=0)

    wr = lax.broadcasted_iota(jnp.int32, (SG_CHUNK, SG_CHUNK), 0)
    wc = lax.broadcasted_iota(jnp.int32, (SG_CHUNK, SG_CHUNK), 1)

    def sg_mix(su):
        for g in range(SG_GROUPS):
            u_g = _gelu(cols(su, g))
            w_g = jnp.where(wr >= wc, ws_ref[g], jnp.zeros((), BF16))
            bias = bst_ref[:, g:g + 1]
            for p in range(tt // SG_CHUNK):
                rs = slice(p * SG_CHUNK, (p + 1) * SG_CHUNK)
                mixed = _dot(w_g, y[g][rs]) + bias
                mix_ref[base + p * SG_CHUNK:base + (p + 1) * SG_CHUNK,
                        GLA_WIDTH + g * SG_CH:GLA_WIDTH + (g + 1) * SG_CH] = (
                            u_g[rs] * mixed).astype(BF16)

    su = [proj_wide(C_SU, j) for j in range(SG_WIDTH // PROJ_COLS)]
    go = []
    acc = []
    fills = [lambda: sg_mix(su),
             lambda: go.append(proj_wide(C_GOUT, 0)),
             lambda: go.append(proj_wide(C_GOUT, 1)),
             lambda: acc.append(x + _dot(mix_ref[rows, GLA_WIDTH:], wout_ref[GLA_WIDTH:, :]))]
    assert len(fills) == n_pairs
    staged = gla_scores(0)
    for p in range(n_pairs):
        fills[p]()
        nxt = gla_scores(p + 1) if p + 1 < n_pairs else None
        st = gla_apply(p, st, *staged)
        staged = nxt

    acc = acc[0]
    for h in range(GLA_HEADS):
        hs = slice(h * GLA_DV, (h + 1) * GLA_DV)
        o_h = _rms(ogla_ref[rows, hs], gnw_ref[...])
        g_h = cols(go, h)
        mix_ref[rows, hs] = (o_h * (g_h * jax.nn.sigmoid(g_h))).astype(BF16)
    o_ref[0, rows, :] = acc + _dot(mix_ref[rows, :GLA_WIDTH], wout_ref[:GLA_WIDTH, :])
    return st


def _ffn_kernel(h_ref, n2w_ref, wg_ref, wu_ref, wd_ref, fnw_ref, o_ref, hid_ref):
    n_sub = h_ref.shape[0] // FFN_SUB
    n_chunks = D_FF // FF_CHUNK
    subs = [slice(s * FFN_SUB, (s + 1) * FFN_SUB) for s in range(n_sub)]
    n2 = [None] * n_sub

    def gate_up(s, f):
        if n2[s] is None:
            n2[s] = _rms(h_ref[subs[s], :], n2w_ref[...]).astype(BF16)
        fs = slice(f * FF_CHUNK, (f + 1) * FF_CHUNK)
        a = _dot(n2[s], wg_ref[:, fs])
        u = _dot(n2[s], wu_ref[:, fs])
        hid_ref[subs[s], fs] = (a * jax.nn.sigmoid(a) * u).astype(BF16)

    def down(s):
        piece = FFN_SUB // DOWN_SPLIT
        for j in range(DOWN_SPLIT):
            rows = slice(subs[s].start + j * piece, subs[s].start + (j + 1) * piece)
            h2 = h_ref[rows, :] + _dot(hid_ref[rows, :], wd_ref[...])
            o_ref[rows, :] = _rms(h2, fnw_ref[...])

    for s in range(n_sub):
        for f in range(FFN_LEAD if s else 0, n_chunks):
            gate_up(s, f)
        if s + 1 < n_sub:
            for f in range(FFN_LEAD):
                gate_up(s + 1, f)
        down(s)


def _const_spec(shape):
    nd = len(shape)
    return pl.BlockSpec(shape, lambda *_: (0,) * nd, pipeline_mode=pl.Buffered(1))


def _mixer(x, n1w, wa, wb, wgu, bgu, gnw, lnw, lnb, ws, bst, wout, ffn_ws):
    bsz, seq, _ = x.shape
    assert seq % MIX_TILE == 0 and MIX_TILE % MIX_SUB == 0 and MIX_SUB % PAIR == 0
    n_t = seq // MIX_TILE
    consts = (n1w, wa, wb, wgu, bgu, gnw, lnw, lnb, ws, bst, wout)

    def slab_spec(w):
        rows, width = w.shape
        slab = next(r for r in range(BF16_SUBLANES, rows + 1, BF16_SUBLANES)
                    if rows % r == 0 and r * bsz * n_t >= rows)
        last = rows // slab - 1
        return pl.BlockSpec((slab, width), lambda b, t: (jnp.minimum(b * n_t + t, last), 0))

    slab_specs = [slab_spec(w) for w in ffn_ws]
    tile_spec = pl.BlockSpec((1, MIX_TILE, D_MODEL), lambda b, t: (b, t, 0))
    h, *narrowed = pl.pallas_call(
        _mixer_kernel,
        grid=(bsz, n_t),
        in_specs=[tile_spec] + [_const_spec(a.shape) for a in consts] + slab_specs,
        out_specs=[tile_spec] + slab_specs,
        out_shape=[jax.ShapeDtypeStruct(x.shape, F32)]
        + [jax.ShapeDtypeStruct(w.shape, BF16) for w in ffn_ws],
        scratch_shapes=[
            pltpu.VMEM((GLA_KEY, GLA_DV), F32),
            pltpu.VMEM((MIX_TILE, GLA_WIDTH), F32),
            pltpu.VMEM((MIX_TILE, D_MODEL), BF16),
            pltpu.VMEM((D_MODEL, GLA_KEY), BF16),
        ],
        compiler_params=pltpu.CompilerParams(
            dimension_semantics=("arbitrary", "arbitrary"),
            vmem_limit_bytes=VMEM_LIMIT),
        name="mixer",
    )(x, *consts, *ffn_ws)
    return h, narrowed


def _ffn(h, n2w, wg, wu, wd, fnw):
    m = h.shape[0]
    assert m % FFN_TILE == 0
    consts = (n2w, wg, wu, wd, fnw)
    return pl.pallas_call(
        _ffn_kernel,
        grid=(m // FFN_TILE,),
        in_specs=[pl.BlockSpec((FFN_TILE, D_MODEL), lambda i: (i, 0))]
        + [_const_spec(a.shape) for a in consts],
        out_specs=pl.BlockSpec((FFN_TILE, D_MODEL), lambda i: (i, 0)),
        out_shape=jax.ShapeDtypeStruct(h.shape, F32),
        scratch_shapes=[pltpu.VMEM((FFN_TILE, D_FF), BF16)],
        compiler_params=pltpu.CompilerParams(
            dimension_semantics=("parallel",),
            vmem_limit_bytes=VMEM_LIMIT),
        name="ffn",
    )(h, *consts)


def kernel(x, norm1_w, w_in, w_gate_up, b_gate_up, gla_norm_w, sg_ln_w, sg_ln_b, sg_w_s, sg_b_s,
           w_out, norm2_w, w_ffn_gate, w_ffn_up, w_ffn_down, final_norm_w):
    bsz, seq, _ = x.shape
    assert w_in.shape[0] == 1, "the final norm is fused into the (single) layer's ffn call"
    wi = w_in[0]
    o_glr = GLA_KEY + GLA_KEY + GLA_WIDTH + GATE_RANK
    wa = jnp.pad(wi[:, :o_glr], ((0, 0), (0, GLR_PAD - GATE_RANK))).astype(BF16)
    wb = wi[:, o_glr:].astype(BF16)
    wgu = jnp.concatenate(
        [w_gate_up[0], jnp.zeros((GLR_PAD - GATE_RANK, GLA_KEY), w_gate_up.dtype)],
        axis=0).astype(BF16)
    h, (wg, wu, wd) = _mixer(
        x, norm1_w[0][None, :], wa, wb, wgu, b_gate_up[0][None, :], gla_norm_w[0][None, :],
        sg_ln_w[0], sg_ln_b[0], sg_w_s[0].astype(BF16), jnp.transpose(sg_b_s[0]),
        w_out[0].astype(BF16), (w_ffn_gate[0], w_ffn_up[0], w_ffn_down[0]))
    out = _ffn(h.reshape(bsz * seq, D_MODEL), norm2_w[0][None, :], wg, wu, wd,
               final_norm_w[None, :])
    return out.reshape(bsz, seq, D_MODEL)
```

```python
import jax
import jax.numpy as jnp
from jax import lax
from jax.experimental import pallas as pl
from jax.experimental.pallas import tpu as pltpu

F32 = jnp.float32
BF16 = jnp.bfloat16

D_MODEL = 1024
GLA_HEADS = 4
GLA_DK = 64
GLA_DV = 128
GLA_KEY = GLA_HEADS * GLA_DK
GLA_WIDTH = GLA_HEADS * GLA_DV
GATE_RANK = 16
GATE_NORMALIZER = 16.0
GLA_CHUNK = 64
SG_GROUPS = 4
SG_CH = 128
SG_WIDTH = SG_GROUPS * SG_CH
SG_CHUNK = 128
D_FF = 2816
NORM_EPS = 1e-5

LANES = 128
BF16_SUBLANES = 16
GLR_PAD = LANES

C_Q = 0
C_K = C_Q + GLA_KEY
C_V = C_K + GLA_KEY
C_GLR = C_V + GLA_WIDTH
A_COLS = C_GLR + GLR_PAD
C_GOUT = A_COLS
C_SU = C_GOUT + GLA_WIDTH
C_SV = C_SU + SG_WIDTH

V_N1W, V_BGU, V_GNW, V_LNW, V_LNB = 0, 1, 2, 3, 3 + SG_GROUPS
VEC_ROWS = 16

PAIR = 2 * GLA_CHUNK
HEADS_PER_GROUP = LANES // GLA_DK
PROJ_COLS = 256

MIX_TILE = 2048
MIX_SUB = 512
FFN_TILE = 1024
FFN_SUB = 512
FF_CHUNK = 256
DOWN_SPLIT = 2
FFN_LEAD = 1
VMEM_LIMIT = 56 * 1024 * 1024


def _rms(x, w):
    ms = jnp.mean(x * x, axis=-1, keepdims=True)
    return x * lax.rsqrt(ms + NORM_EPS) * w


def _gelu(x):
    return 0.5 * x * (1.0 + lax.erf(x * (0.5 ** 0.5)))


def _dot(a, b):
    return jnp.dot(a, b, preferred_element_type=F32)


def _dot_nt(a, b):
    return lax.dot_general(a, b, (((1,), (1,)), ((), ())), preferred_element_type=F32)


def _mixer_kernel(x_ref, vec_ref, wa_ref, wb_ref, wgu_ref, ws_ref, bst_ref, wout_ref,
                  f0_ref, f1_ref, f2_ref,
                  o_ref, f0b_ref, f1b_ref, f2b_ref, st_ref, ogla_ref, mix_ref, wfold_ref):
    @pl.when(pl.program_id(1) == 0)
    def _():
        st_ref[...] = jnp.zeros_like(st_ref)

    @pl.when((pl.program_id(0) == 0) & (pl.program_id(1) == 0))
    def _():
        wfold_ref[...] = _dot(wa_ref[:, C_GLR:C_GLR + GLR_PAD], wgu_ref[...]).astype(BF16)

    st = st_ref[...]
    for s in range(x_ref.shape[1] // MIX_SUB):
        st = _mixer_sub_tile(s * MIX_SUB, st, x_ref, vec_ref, wa_ref, wb_ref, wfold_ref, ws_ref,
                             bst_ref, wout_ref, o_ref, ogla_ref, mix_ref)
    st_ref[...] = st

    for src, dst in ((f0_ref, f0b_ref), (f1_ref, f1b_ref), (f2_ref, f2b_ref)):
        dst[...] = src[...].astype(BF16)


def _mixer_sub_tile(base, st, x_ref, vec_ref, wa_ref, wb_ref, wfold_ref, ws_ref, bst_ref, wout_ref,
                    o_ref, ogla_ref, mix_ref):
    tt = MIX_SUB
    n_pairs = tt // PAIR
    per = PROJ_COLS // LANES
    rows = slice(base, base + tt)

    x = x_ref[0, rows, :]
    n = _rms(x, vec_ref[V_N1W:V_N1W + 1, :]).astype(BF16)

    def proj(lo, width):
        if lo < A_COLS:
            return _dot(n, wa_ref[:, lo:lo + width])
        return _dot(n, wb_ref[:, lo - A_COLS:lo - A_COLS + width])

    def proj_wide(lo, j):
        return proj(lo + j * PROJ_COLS, PROJ_COLS)

    def cols(pieces, g):
        return pieces[g // per][:, (g % per) * LANES:(g % per + 1) * LANES]

    gk = _dot(n, wfold_ref[...]) + vec_ref[V_BGU:V_BGU + 1, :GLA_KEY]
    sv = [proj_wide(C_SV, j) for j in range(SG_WIDTH // PROJ_COLS)]
    log_a = (jnp.minimum(gk, 0.0) - jnp.log1p(jnp.exp(-jnp.abs(gk)))) * (1.0 / GATE_NORMALIZER)
    k = proj(C_K, GLA_KEY)
    q = proj(C_Q, GLA_KEY)

    la_hi = log_a.astype(BF16)
    la_lo = (log_a - la_hi.astype(F32)).astype(BF16)
    r = lax.broadcasted_iota(jnp.int32, (PAIR, 2 * PAIR), 0)
    c = lax.broadcasted_iota(jnp.int32, (PAIR, 2 * PAIR), 1) % PAIR
    tri2 = (r >= c).astype(BF16)
    b_parts = []
    for p in range(n_pairs):
        psl = slice(p * PAIR, (p + 1) * PAIR)
        b_parts.append(_dot(tri2, jnp.concatenate([la_hi[psl], la_lo[psl]], axis=0)))
    v = proj(C_V, GLA_WIDTH)

    y = []
    for g in range(SG_GROUPS):
        v_g = _gelu(cols(sv, g))
        mu = jnp.mean(v_g, axis=-1, keepdims=True)
        d = v_g - mu
        var = jnp.mean(d * d, axis=-1, keepdims=True)
        y.append((d * lax.rsqrt(var + NORM_EPS) * vec_ref[V_LNW + g:V_LNW + g + 1, :SG_CH]
                  + vec_ref[V_LNB + g:V_LNB + g + 1, :SG_CH]).astype(BF16))

    lane_head = lax.broadcasted_iota(jnp.int32, (PAIR, GLA_KEY), 1) // GLA_DK
    pr = lax.broadcasted_iota(jnp.int32, (GLA_HEADS * PAIR, PAIR), 0) % PAIR
    pc = lax.broadcasted_iota(jnp.int32, (GLA_HEADS * PAIR, PAIR), 1)
    causal = pr >= pc

    def gla_scores(p):
        psl = slice(p * PAIR, (p + 1) * PAIR)
        b = b_parts[p]
        b_mid = b[GLA_CHUNK - 1:GLA_CHUNK]
        b_end = b[PAIR - 1:PAIR]
        qs = q[psl] * (GLA_DK ** -0.5)
        kp = k[psl]
        q_rel = qs * jnp.exp(b - b_mid)
        k_rel = (kp * jnp.exp(b_mid - b)).astype(BF16)
        q_in = qs * jnp.exp(b)
        k_out_t = (kp * jnp.exp(b_end - b)).T.astype(BF16)
        decay_t = jnp.exp(jnp.broadcast_to(b_end, (LANES, GLA_KEY)).T)
        scores = []
        for g in range(GLA_HEADS // HEADS_PER_GROUP):
            lanes = slice(g * LANES, (g + 1) * LANES)
            q_stack = jnp.concatenate(
                [jnp.where(lane_head == h, q_rel, 0.0)[:, lanes]
                 for h in range(g * HEADS_PER_GROUP, (g + 1) * HEADS_PER_GROUP)], axis=0)
            scores.append(_dot_nt(q_stack.astype(BF16), k_rel[:, lanes]))
        probs = jnp.where(causal, jnp.concatenate(scores, axis=0), 0.0).astype(BF16)
        q_heads = [jnp.where(lane_head == h, q_in, 0.0)[:, (h // HEADS_PER_GROUP) * LANES:
                                                         (h // HEADS_PER_GROUP + 1) * LANES].astype(BF16)
                   for h in range(GLA_HEADS)]
        return probs, q_heads, k_out_t, decay_t

    def gla_apply(p, st, probs, q_heads, k_out_t, decay_t):
        vp_b = v[p * PAIR:(p + 1) * PAIR].astype(BF16)
        st_b = st.astype(BF16)
        zeros = jnp.zeros((GLA_DK, LANES), BF16)
        o, u = [], []
        for h in range(GLA_HEADS):
            g = h // HEADS_PER_GROUP
            v_h = vp_b[:, h * GLA_DV:(h + 1) * GLA_DV]
            lhs = jnp.concatenate(
                [jnp.concatenate([probs[h * PAIR:(h + 1) * PAIR], q_heads[h]], axis=1),
                 jnp.concatenate([k_out_t[h * GLA_DK:(h + 1) * GLA_DK], zeros], axis=1)], axis=0)
            rhs = jnp.concatenate([v_h, st_b[g * LANES:(g + 1) * LANES]], axis=0)
            ou = _dot(lhs, rhs)
            o.append(ou[:PAIR])
            u.append(ou[PAIR:])
        ogla_ref[base + p * PAIR:base + (p + 1) * PAIR, :] = jnp.concatenate(o, axis=1)
        return st * decay_t + jnp.concatenate(u, axis=0)

    wr = lax.broadcasted_iota(jnp.int32, (SG_CHUNK, SG_CHUNK), 0)
    wc = lax.broadcasted_iota(jnp.int32, (SG_CHUNK, SG_CHUNK), 1)

    def sg_mix(su):
        for g in range(SG_GROUPS):
            u_g = _gelu(cols(su, g))
            w_g = jnp.where(wr >= wc, ws_ref[g], jnp.zeros((), BF16))
            bias = bst_ref[:, g:g + 1]
            for p in range(tt // SG_CHUNK):
                rs = slice(p * SG_CHUNK, (p + 1) * SG_CHUNK)
                mixed = _dot(w_g, y[g][rs]) + bias
                mix_ref[base + p * SG_CHUNK:base + (p + 1) * SG_CHUNK,
                        GLA_WIDTH + g * SG_CH:GLA_WIDTH + (g + 1) * SG_CH] = (
                            u_g[rs] * mixed).astype(BF16)

    su = [proj_wide(C_SU, j) for j in range(SG_WIDTH // PROJ_COLS)]
    go = []
    acc = []
    fills = [lambda: sg_mix(su),
             lambda: go.append(proj_wide(C_GOUT, 0)),
             lambda: go.append(proj_wide(C_GOUT, 1)),
             lambda: acc.append(x + _dot(mix_ref[rows, GLA_WIDTH:], wout_ref[GLA_WIDTH:, :]))]
    assert len(fills) == n_pairs
    staged = gla_scores(0)
    for p in range(n_pairs):
        fills[p]()
        nxt = gla_scores(p + 1) if p + 1 < n_pairs else None
        st = gla_apply(p, st, *staged)
        staged = nxt

    acc = acc[0]
    for h in range(GLA_HEADS):
        hs = slice(h * GLA_DV, (h + 1) * GLA_DV)
        o_h = _rms(ogla_ref[rows, hs], vec_ref[V_GNW:V_GNW + 1, :GLA_DV])
        g_h = cols(go, h)
        mix_ref[rows, hs] = (o_h * (g_h * jax.nn.sigmoid(g_h))).astype(BF16)
    o_ref[0, rows, :] = acc + _dot(mix_ref[rows, :GLA_WIDTH], wout_ref[:GLA_WIDTH, :])
    return st


def _ffn_kernel(h_ref, n2w_ref, wg_ref, wu_ref, wd_ref, fnw_ref, o_ref, hid_ref):
    n_sub = h_ref.shape[0] // FFN_SUB
    n_chunks = D_FF // FF_CHUNK
    subs = [slice(s * FFN_SUB, (s + 1) * FFN_SUB) for s in range(n_sub)]
    n2 = [None] * n_sub

    def gate_up(s, f):
        if n2[s] is None:
            n2[s] = _rms(h_ref[subs[s], :], n2w_ref[...]).astype(BF16)
        fs = slice(f * FF_CHUNK, (f + 1) * FF_CHUNK)
        a = _dot(n2[s], wg_ref[:, fs])
        u = _dot(n2[s], wu_ref[:, fs])
        hid_ref[subs[s], fs] = (a * jax.nn.sigmoid(a) * u).astype(BF16)

    def down(s):
        piece = FFN_SUB // DOWN_SPLIT
        for j in range(DOWN_SPLIT):
            rows = slice(subs[s].start + j * piece, subs[s].start + (j + 1) * piece)
            h2 = h_ref[rows, :] + _dot(hid_ref[rows, :], wd_ref[...])
            o_ref[rows, :] = _rms(h2, fnw_ref[...])

    for s in range(n_sub):
        for f in range(FFN_LEAD if s else 0, n_chunks):
            gate_up(s, f)
        if s + 1 < n_sub:
            for f in range(FFN_LEAD):
                gate_up(s + 1, f)
        down(s)


def _const_spec(shape):
    nd = len(shape)
    return pl.BlockSpec(shape, lambda *_: (0,) * nd, pipeline_mode=pl.Buffered(1))


def _mixer(x, vecs, wa, wb, wgu, ws, bst, wout, ffn_ws):
    bsz, seq, _ = x.shape
    assert seq % MIX_TILE == 0 and MIX_TILE % MIX_SUB == 0 and MIX_SUB % PAIR == 0
    n_t = seq // MIX_TILE
    consts = (vecs, wa, wb, wgu, ws, bst, wout)

    def slab_spec(w):
        rows, width = w.shape
        slab = next(r for r in range(BF16_SUBLANES, rows + 1, BF16_SUBLANES)
                    if rows % r == 0 and r * bsz * n_t >= rows)
        last = rows // slab - 1
        return pl.BlockSpec((slab, width), lambda b, t: (jnp.minimum(b * n_t + t, last), 0))

    slab_specs = [slab_spec(w) for w in ffn_ws]
    tile_spec = pl.BlockSpec((1, MIX_TILE, D_MODEL), lambda b, t: (b, t, 0))
    h, *narrowed = pl.pallas_call(
        _mixer_kernel,
        grid=(bsz, n_t),
        in_specs=[tile_spec] + [_const_spec(a.shape) for a in consts] + slab_specs,
        out_specs=[tile_spec] + slab_specs,
        out_shape=[jax.ShapeDtypeStruct(x.shape, F32)]
        + [jax.ShapeDtypeStruct(w.shape, BF16) for w in ffn_ws],
        scratch_shapes=[
            pltpu.VMEM((GLA_KEY, GLA_DV), F32),
            pltpu.VMEM((MIX_TILE, GLA_WIDTH), F32),
            pltpu.VMEM((MIX_TILE, D_MODEL), BF16),
            pltpu.VMEM((D_MODEL, GLA_KEY), BF16),
        ],
        compiler_params=pltpu.CompilerParams(
            dimension_semantics=("arbitrary", "arbitrary"),
            vmem_limit_bytes=VMEM_LIMIT),
        name="mixer",
    )(x, *consts, *ffn_ws)
    return h, narrowed


def _ffn(h, n2w, wg, wu, wd, fnw):
    m = h.shape[0]
    assert m % FFN_TILE == 0
    consts = (n2w, wg, wu, wd, fnw)
    return pl.pallas_call(
        _ffn_kernel,
        grid=(m // FFN_TILE,),
        in_specs=[pl.BlockSpec((FFN_TILE, D_MODEL), lambda i: (i, 0))]
        + [_const_spec(a.shape) for a in consts],
        out_specs=pl.BlockSpec((FFN_TILE, D_MODEL), lambda i: (i, 0)),
        out_shape=jax.ShapeDtypeStruct(h.shape, F32),
        scratch_shapes=[pltpu.VMEM((FFN_TILE, D_FF), BF16)],
        compiler_params=pltpu.CompilerParams(
            dimension_semantics=("parallel",),
            vmem_limit_bytes=VMEM_LIMIT),
        name="ffn",
    )(h, *consts)


def kernel(x, norm1_w, w_in, w_gate_up, b_gate_up, gla_norm_w, sg_ln_w, sg_ln_b, sg_w_s, sg_b_s,
           w_out, norm2_w, w_ffn_gate, w_ffn_up, w_ffn_down, final_norm_w):
    bsz, seq, _ = x.shape
    assert w_in.shape[0] == 1, "the final norm is fused into the (single) layer's ffn call"
    wi = w_in[0]
    o_glr = GLA_KEY + GLA_KEY + GLA_WIDTH + GATE_RANK
    wa = jnp.pad(wi[:, :o_glr], ((0, 0), (0, GLR_PAD - GATE_RANK))).astype(BF16)
    wb = wi[:, o_glr:].astype(BF16)
    wgu = jnp.concatenate(
        [w_gate_up[0], jnp.zeros((GLR_PAD - GATE_RANK, GLA_KEY), w_gate_up.dtype)],
        axis=0).astype(BF16)
    def row(vec):
        return jnp.pad(vec, (0, D_MODEL - vec.shape[0]))[None, :]

    vec_rows = ([row(norm1_w[0]), row(b_gate_up[0]), row(gla_norm_w[0])]
                + [row(sg_ln_w[0][g]) for g in range(SG_GROUPS)]
                + [row(sg_ln_b[0][g]) for g in range(SG_GROUPS)])
    vecs = jnp.concatenate(
        vec_rows + [jnp.zeros((VEC_ROWS - len(vec_rows), D_MODEL), F32)], axis=0)
    h, (wg, wu, wd) = _mixer(
        x, vecs, wa, wb, wgu, sg_w_s[0].astype(BF16), jnp.transpose(sg_b_s[0]),
        w_out[0].astype(BF16), (w_ffn_gate[0], w_ffn_up[0], w_ffn_down[0]))
    out = _ffn(h.reshape(bsz * seq, D_MODEL), norm2_w[0][None, :], wg, wu, wd,
               final_norm_w[None, :])
    return out.reshape(bsz, seq, D_MODEL)
```

```python
import jax
import jax.numpy as jnp
from jax import lax
from jax.experimental import pallas as pl
from jax.experimental.pallas import tpu as pltpu

F32 = jnp.float32
BF16 = jnp.bfloat16

D_MODEL = 1024
GLA_HEADS = 4
GLA_DK = 64
GLA_DV = 128
GLA_KEY = GLA_HEADS * GLA_DK
GLA_WIDTH = GLA_HEADS * GLA_DV
GATE_RANK = 16
GATE_NORMALIZER = 16.0
GLA_CHUNK = 64
SG_GROUPS = 4
SG_CH = 128
SG_WIDTH = SG_GROUPS * SG_CH
SG_CHUNK = 128
D_FF = 2816
NORM_EPS = 1e-5

LANES = 128
BF16_SUBLANES = 16
GLR_PAD = LANES

C_Q = 0
C_K = C_Q + GLA_KEY
C_V = C_K + GLA_KEY
C_GLR = C_V + GLA_WIDTH
A_COLS = C_GLR + GLR_PAD
C_GOUT = A_COLS
C_SU = C_GOUT + GLA_WIDTH
C_SV = C_SU + SG_WIDTH

PAIR = 2 * GLA_CHUNK
HEADS_PER_GROUP = LANES // GLA_DK
PROJ_COLS = 256

MIX_TILE = 2048
MIX_SUB = 512
FFN_TILE = 1024
FFN_SUB = 512
FF_CHUNK = 256
DOWN_SPLIT = 2
FFN_LEAD = 1
VMEM_LIMIT = 56 * 1024 * 1024


def _rms(x, w):
    ms = jnp.mean(x * x, axis=-1, keepdims=True)
    return x * lax.rsqrt(ms + NORM_EPS) * w


def _gelu(x):
    return 0.5 * x * (1.0 + lax.erf(x * (0.5 ** 0.5)))


def _dot(a, b):
    return jnp.dot(a, b, preferred_element_type=F32)


def _mixer_kernel(x_ref, n1w_ref, wa_ref, wb_ref, wgu_ref, bgu_ref, gnw_ref, lnw_ref, lnb_ref,
                  ws_ref, bst_ref, wout_ref, f0_ref, f1_ref, f2_ref,
                  o_ref, f0b_ref, f1b_ref, f2b_ref, st_ref, ogla_ref, mix_ref, wfold_ref):
    @pl.when(pl.program_id(1) == 0)
    def _():
        st_ref[...] = jnp.zeros_like(st_ref)

    @pl.when((pl.program_id(0) == 0) & (pl.program_id(1) == 0))
    def _():
        wfold_ref[...] = _dot(wa_ref[:, C_GLR:C_GLR + GLR_PAD], wgu_ref[...]).astype(BF16)

    st = st_ref[...]
    for s in range(x_ref.shape[1] // MIX_SUB):
        st = _mixer_sub_tile(s * MIX_SUB, st, x_ref, n1w_ref, wa_ref, wb_ref, wfold_ref, bgu_ref,
                             gnw_ref, lnw_ref, lnb_ref, ws_ref, bst_ref, wout_ref, o_ref, ogla_ref,
                             mix_ref)
    st_ref[...] = st

    for src, dst in ((f0_ref, f0b_ref), (f1_ref, f1b_ref), (f2_ref, f2b_ref)):
        dst[...] = src[...].astype(BF16)


def _mixer_sub_tile(base, st, x_ref, n1w_ref, wa_ref, wb_ref, wfold_ref, bgu_ref, gnw_ref, lnw_ref,
                    lnb_ref, ws_ref, bst_ref, wout_ref, o_ref, ogla_ref, mix_ref):
    tt = MIX_SUB
    n_pairs = tt // PAIR
    per = PROJ_COLS // LANES
    rows = slice(base, base + tt)

    x = x_ref[0, rows, :]
    n = _rms(x, n1w_ref[...]).astype(BF16)

    def proj(lo, width):
        if lo < A_COLS:
            return _dot(n, wa_ref[:, lo:lo + width])
        return _dot(n, wb_ref[:, lo - A_COLS:lo - A_COLS + width])

    def proj_wide(lo, j):
        return proj(lo + j * PROJ_COLS, PROJ_COLS)

    def cols(pieces, g):
        return pieces[g // per][:, (g % per) * LANES:(g % per + 1) * LANES]

    gk = _dot(n, wfold_ref[...]) + bgu_ref[...]
    sv = [proj_wide(C_SV, j) for j in range(SG_WIDTH // PROJ_COLS)]
    log_a = (jnp.minimum(gk, 0.0) - jnp.log1p(jnp.exp(-jnp.abs(gk)))) * (1.0 / GATE_NORMALIZER)
    k = proj(C_K, GLA_KEY)
    q = proj(C_Q, GLA_KEY)

    la_hi = log_a.astype(BF16)
    la_lo = (log_a - la_hi.astype(F32)).astype(BF16)
    r = lax.broadcasted_iota(jnp.int32, (PAIR, 2 * PAIR), 0)
    c = lax.broadcasted_iota(jnp.int32, (PAIR, 2 * PAIR), 1) % PAIR
    tri2 = (r >= c).astype(BF16)
    b_parts = []
    for p in range(n_pairs):
        psl = slice(p * PAIR, (p + 1) * PAIR)
        b_parts.append(_dot(tri2, jnp.concatenate([la_hi[psl], la_lo[psl]], axis=0)))
    v = proj(C_V, GLA_WIDTH)

    y = []
    for g in range(SG_GROUPS):
        v_g = _gelu(cols(sv, g))
        mu = jnp.mean(v_g, axis=-1, keepdims=True)
        d = v_g - mu
        var = jnp.mean(d * d, axis=-1, keepdims=True)
        y.append((d * lax.rsqrt(var + NORM_EPS) * lnw_ref[g:g + 1, :]
                  + lnb_ref[g:g + 1, :]).astype(BF16))

    lane_head = lax.broadcasted_iota(jnp.int32, (PAIR, GLA_KEY), 1) // GLA_DK
    pr = lax.broadcasted_iota(jnp.int32, (GLA_HEADS * PAIR, PAIR), 0) % PAIR
    pc = lax.broadcasted_iota(jnp.int32, (GLA_HEADS * PAIR, PAIR), 1)
    causal = pr >= pc

    def gla_scores(p):
        psl = slice(p * PAIR, (p + 1) * PAIR)
        b = b_parts[p]
        b_mid = b[GLA_CHUNK - 1:GLA_CHUNK]
        b_end = b[PAIR - 1:PAIR]
        qs = q[psl] * (GLA_DK ** -0.5)
        kp = k[psl]
        q_rel = qs * jnp.exp(b - b_mid)
        k_rel_t = (kp * jnp.exp(b_mid - b)).T.astype(BF16)
        q_in = qs * jnp.exp(b)
        k_out_t = (kp * jnp.exp(b_end - b)).T.astype(BF16)
        decay_t = jnp.exp(jnp.broadcast_to(b_end, (LANES, GLA_KEY)).T)
        scores = []
        for g in range(GLA_HEADS // HEADS_PER_GROUP):
            lanes = slice(g * LANES, (g + 1) * LANES)
            q_stack = jnp.concatenate(
                [jnp.where(lane_head == h, q_rel, 0.0)[:, lanes]
                 for h in range(g * HEADS_PER_GROUP, (g + 1) * HEADS_PER_GROUP)], axis=0)
            scores.append(_dot(q_stack.astype(BF16), k_rel_t[lanes]))
        probs = jnp.where(causal, jnp.concatenate(scores, axis=0), 0.0).astype(BF16)
        q_heads = [jnp.where(lane_head == h, q_in, 0.0)[:, (h // HEADS_PER_GROUP) * LANES:
                                                         (h // HEADS_PER_GROUP + 1) * LANES].astype(BF16)
                   for h in range(GLA_HEADS)]
        return probs, q_heads, k_out_t, decay_t

    def gla_apply(p, st, probs, q_heads, k_out_t, decay_t):
        vp_b = v[p * PAIR:(p + 1) * PAIR].astype(BF16)
        st_b = st.astype(BF16)
        zeros = jnp.zeros((GLA_DK, LANES), BF16)
        o, u = [], []
        for h in range(GLA_HEADS):
            g = h // HEADS_PER_GROUP
            v_h = vp_b[:, h * GLA_DV:(h + 1) * GLA_DV]
            lhs = jnp.concatenate(
                [jnp.concatenate([probs[h * PAIR:(h + 1) * PAIR], q_heads[h]], axis=1),
                 jnp.concatenate([k_out_t[h * GLA_DK:(h + 1) * GLA_DK], zeros], axis=1)], axis=0)
            rhs = jnp.concatenate([v_h, st_b[g * LANES:(g + 1) * LANES]], axis=0)
            ou = _dot(lhs, rhs)
            o.append(ou[:PAIR])
            u.append(ou[PAIR:])
        ogla_ref[base + p * PAIR:base + (p + 1) * PAIR, :] = jnp.concatenate(o, axis=1)
        return st * decay_t + jnp.concatenate(u, axis=0)

    wr = lax.broadcasted_iota(jnp.int32, (SG_CHUNK, SG_CHUNK), 0)
    wc = lax.broadcasted_iota(jnp.int32, (SG_CHUNK, SG_CHUNK), 1)

    def sg_mix(su):
        for g in range(SG_GROUPS):
            u_g = _gelu(cols(su, g))
            w_g = jnp.where(wr >= wc, ws_ref[g], jnp.zeros((), BF16))
            bias = bst_ref[:, g:g + 1]
            for p in range(tt // SG_CHUNK):
                rs = slice(p * SG_CHUNK, (p + 1) * SG_CHUNK)
                mixed = _dot(w_g, y[g][rs]) + bias
                mix_ref[base + p * SG_CHUNK:base + (p + 1) * SG_CHUNK,
                        GLA_WIDTH + g * SG_CH:GLA_WIDTH + (g + 1) * SG_CH] = (
                            u_g[rs] * mixed).astype(BF16)

    su = [proj_wide(C_SU, j) for j in range(SG_WIDTH // PROJ_COLS)]
    go = []
    acc = []
    fills = [lambda: sg_mix(su),
             lambda: go.append(proj_wide(C_GOUT, 0)),
             lambda: go.append(proj_wide(C_GOUT, 1)),
             lambda: acc.append(x + _dot(mix_ref[rows, GLA_WIDTH:], wout_ref[GLA_WIDTH:, :]))]
    assert len(fills) == n_pairs
    staged = gla_scores(0)
    for p in range(n_pairs):
        fills[p]()
        nxt = gla_scores(p + 1) if p + 1 < n_pairs else None
        st = gla_apply(p, st, *staged)
        staged = nxt

    acc = acc[0]
    for h in range(GLA_HEADS):
        hs = slice(h * GLA_DV, (h + 1) * GLA_DV)
        o_h = _rms(ogla_ref[rows, hs], gnw_ref[...])
        g_h = cols(go, h)
        mix_ref[rows, hs] = (o_h * (g_h * jax.nn.sigmoid(g_h))).astype(BF16)
    o_ref[0, rows, :] = acc + _dot(mix_ref[rows, :GLA_WIDTH], wout_ref[:GLA_WIDTH, :])
    return st


def _ffn_kernel(h_ref, n2w_ref, wg_ref, wu_ref, wd_ref, fnw_ref, o_ref, hid_ref):
    n_sub = h_ref.shape[0] // FFN_SUB
    n_chunks = D_FF // FF_CHUNK
    subs = [slice(s * FFN_SUB, (s + 1) * FFN_SUB) for s in range(n_sub)]
    n2 = [None] * n_sub

    def gate_up(s, f):
        if n2[s] is None:
            n2[s] = _rms(h_ref[subs[s], :], n2w_ref[...]).astype(BF16)
        fs = slice(f * FF_CHUNK, (f + 1) * FF_CHUNK)
        a = _dot(n2[s], wg_ref[:, fs])
        u = _dot(n2[s], wu_ref[:, fs])
        hid_ref[subs[s], fs] = (a * jax.nn.sigmoid(a) * u).astype(BF16)

    def down(s):
        piece = FFN_SUB // DOWN_SPLIT
        for j in range(DOWN_SPLIT):
            rows = slice(subs[s].start + j * piece, subs[s].start + (j + 1) * piece)
            h2 = h_ref[rows, :] + _dot(hid_ref[rows, :], wd_ref[...])
            o_ref[rows, :] = _rms(h2, fnw_ref[...])

    for s in range(n_sub):
        for f in range(FFN_LEAD if s else 0, n_chunks):
            gate_up(s, f)
        if s + 1 < n_sub:
            for f in range(FFN_LEAD):
                gate_up(s + 1, f)
        down(s)


def _const_spec(shape):
    nd = len(shape)
    return pl.BlockSpec(shape, lambda *_: (0,) * nd, pipeline_mode=pl.Buffered(1))


def _mixer(x, n1w, wa, wb, wgu, bgu, gnw, lnw, lnb, ws, bst, wout, ffn_ws):
    bsz, seq, _ = x.shape
    assert seq % MIX_TILE == 0 and MIX_TILE % MIX_SUB == 0 and MIX_SUB % PAIR == 0
    n_t = seq // MIX_TILE
    consts = (n1w, wa, wb, wgu, bgu, gnw, lnw, lnb, ws, bst, wout)

    def slab_spec(w):
        rows, width = w.shape
        slab = next(r for r in range(BF16_SUBLANES, rows + 1, BF16_SUBLANES)
                    if rows % r == 0 and r * bsz * n_t >= rows)
        last = rows // slab - 1
        return pl.BlockSpec((slab, width), lambda b, t: (jnp.minimum(b * n_t + t, last), 0))

    slab_specs = [slab_spec(w) for w in ffn_ws]
    tile_spec = pl.BlockSpec((1, MIX_TILE, D_MODEL), lambda b, t: (b, t, 0))
    h, *narrowed = pl.pallas_call(
        _mixer_kernel,
        grid=(bsz, n_t),
        in_specs=[tile_spec] + [_const_spec(a.shape) for a in consts] + slab_specs,
        out_specs=[tile_spec] + slab_specs,
        out_shape=[jax.ShapeDtypeStruct(x.shape, F32)]
        + [jax.ShapeDtypeStruct(w.shape, BF16) for w in ffn_ws],
        scratch_shapes=[
            pltpu.VMEM((GLA_KEY, GLA_DV), F32),
            pltpu.VMEM((MIX_TILE, GLA_WIDTH), F32),
            pltpu.VMEM((MIX_TILE, D_MODEL), BF16),
            pltpu.VMEM((D_MODEL, GLA_KEY), BF16),
        ],
        compiler_params=pltpu.CompilerParams(
            dimension_semantics=("arbitrary", "arbitrary"),
            vmem_limit_bytes=VMEM_LIMIT),
        name="mixer",
    )(x, *consts, *ffn_ws)
    return h, narrowed


def _ffn(h, n2w, wg, wu, wd, fnw):
    m = h.shape[0]
    assert m % FFN_TILE == 0
    consts = (n2w, wg, wu, wd, fnw)
    return pl.pallas_call(
        _ffn_kernel,
        grid=(m // FFN_TILE,),
        in_specs=[pl.BlockSpec((FFN_TILE, D_MODEL), lambda i: (i, 0))]
        + [_const_spec(a.shape) for a in consts],
        out_specs=pl.BlockSpec((FFN_TILE, D_MODEL), lambda i: (i, 0)),
        out_shape=jax.ShapeDtypeStruct(h.shape, F32),
        scratch_shapes=[pltpu.VMEM((FFN_TILE, D_FF), BF16)],
        compiler_params=pltpu.CompilerParams(
            dimension_semantics=("parallel",),
            vmem_limit_bytes=VMEM_LIMIT),
        name="ffn",
    )(h, *consts)


def kernel(x, norm1_w, w_in, w_gate_up, b_gate_up, gla_norm_w, sg_ln_w, sg_ln_b, sg_w_s, sg_b_s,
           w_out, norm2_w, w_ffn_gate, w_ffn_up, w_ffn_down, final_norm_w):
    bsz, seq, _ = x.shape
    assert w_in.shape[0] == 1, "the final norm is fused into the (single) layer's ffn call"
    wi = w_in[0]
    o_glr = GLA_KEY + GLA_KEY + GLA_WIDTH + GATE_RANK
    wa = jnp.pad(wi[:, :o_glr], ((0, 0), (0, GLR_PAD - GATE_RANK))).astype(BF16)
    wb = wi[:, o_glr:].astype(BF16)
    wgu = jnp.concatenate(
        [w_gate_up[0], jnp.zeros((GLR_PAD - GATE_RANK, GLA_KEY), w_gate_up.dtype)],
        axis=0).astype(BF16)
    h, (wg, wu, wd) = _mixer(
        x, norm1_w[0][None, :], wa, wb, wgu, b_gate_up[0][None, :], gla_norm_w[0][None, :],
        sg_ln_w[0], sg_ln_b[0], sg_w_s[0].astype(BF16), jnp.transpose(sg_b_s[0]),
        w_out[0].astype(BF16), (w_ffn_gate[0], w_ffn_up[0], w_ffn_down[0]))
    out = _ffn(h.reshape(bsz * seq, D_MODEL), norm2_w[0][None, :], wg, wu, wd,
               final_norm_w[None, :])
    return out.reshape(bsz, seq, D_MODEL)
```

```python
import jax
import jax.numpy as jnp
from jax import lax
from jax.experimental import pallas as pl
from jax.experimental.pallas import tpu as pltpu

F32 = jnp.float32
BF16 = jnp.bfloat16

D_MODEL = 1024
GLA_HEADS = 4
GLA_DK = 64
GLA_DV = 128
GLA_KEY = GLA_HEADS * GLA_DK
GLA_WIDTH = GLA_HEADS * GLA_DV
GATE_RANK = 16
GATE_NORMALIZER = 16.0
GLA_CHUNK = 64
SG_GROUPS = 4
SG_CH = 128
SG_WIDTH = SG_GROUPS * SG_CH
SG_CHUNK = 128
D_FF = 2816
NORM_EPS = 1e-5

LANES = 128
BF16_SUBLANES = 16
GLR_PAD = LANES

C_Q = 0
C_K = C_Q + GLA_KEY
C_V = C_K + GLA_KEY
C_GLR = C_V + GLA_WIDTH
A_COLS = C_GLR + GLR_PAD
C_GOUT = A_COLS
C_SU = C_GOUT + GLA_WIDTH
C_SV = C_SU + SG_WIDTH

PAIR = 2 * GLA_CHUNK
HEADS_PER_GROUP = LANES // GLA_DK
PROJ_COLS = 256

MIX_TILE = 2048
MIX_SUB = 512
FFN_TILE = 1024
FFN_SUB = 512
FF_CHUNK = 256
DOWN_SPLIT = 2
FFN_LEAD = 1
VMEM_LIMIT = 56 * 1024 * 1024


def _rms(x, w):
    ms = jnp.mean(x * x, axis=-1, keepdims=True)
    return x * lax.rsqrt(ms + NORM_EPS) * w


def _gelu(x):
    return 0.5 * x * (1.0 + lax.erf(x * (0.5 ** 0.5)))


def _dot(a, b):
    return jnp.dot(a, b, preferred_element_type=F32)


def _mixer_kernel(x_ref, n1w_ref, wa_ref, wb_ref, wgu_ref, bgu_ref, gnw_ref, lnw_ref, lnb_ref,
                  ws_ref, bst_ref, wout_ref, f0_ref, f1_ref, f2_ref,
                  o_ref, f0b_ref, f1b_ref, f2b_ref, st_ref, ogla_ref, mix_ref, wfold_ref):
    @pl.when(pl.program_id(1) == 0)
    def _():
        st_ref[...] = jnp.zeros_like(st_ref)

    @pl.when((pl.program_id(0) == 0) & (pl.program_id(1) == 0))
    def _():
        wfold_ref[...] = _dot(wa_ref[:, C_GLR:C_GLR + GLR_PAD], wgu_ref[...]).astype(BF16)

    st = st_ref[...]
    for s in range(x_ref.shape[1] // MIX_SUB):
        st = _mixer_sub_tile(s * MIX_SUB, st, x_ref, n1w_ref, wa_ref, wb_ref, wfold_ref, bgu_ref,
                             gnw_ref, lnw_ref, lnb_ref, ws_ref, bst_ref, wout_ref, o_ref, ogla_ref,
                             mix_ref)
    st_ref[...] = st

    for src, dst in ((f0_ref, f0b_ref), (f1_ref, f1b_ref), (f2_ref, f2b_ref)):
        dst[...] = src[...].astype(BF16)


def _mixer_sub_tile(base, st, x_ref, n1w_ref, wa_ref, wb_ref, wfold_ref, bgu_ref, gnw_ref, lnw_ref,
                    lnb_ref, ws_ref, bst_ref, wout_ref, o_ref, ogla_ref, mix_ref):
    tt = MIX_SUB
    n_pairs = tt // PAIR
    per = PROJ_COLS // LANES
    rows = slice(base, base + tt)

    x = x_ref[0, rows, :]
    n = _rms(x, n1w_ref[...]).astype(BF16)

    def proj(lo, width):
        if lo < A_COLS:
            return _dot(n, wa_ref[:, lo:lo + width])
        return _dot(n, wb_ref[:, lo - A_COLS:lo - A_COLS + width])

    def proj_wide(lo, j):
        return proj(lo + j * PROJ_COLS, PROJ_COLS)

    def cols(pieces, g):
        return pieces[g // per][:, (g % per) * LANES:(g % per + 1) * LANES]

    gk = _dot(n, wfold_ref[...]) + bgu_ref[...]
    sv = [proj_wide(C_SV, j) for j in range(SG_WIDTH // PROJ_COLS)]
    log_a = (jnp.minimum(gk, 0.0) - jnp.log1p(jnp.exp(-jnp.abs(gk)))) * (1.0 / GATE_NORMALIZER)
    k = proj(C_K, GLA_KEY)
    q = proj(C_Q, GLA_KEY)

    la_hi = log_a.astype(BF16)
    la_lo = (log_a - la_hi.astype(F32)).astype(BF16)
    r = lax.broadcasted_iota(jnp.int32, (PAIR, 2 * PAIR), 0)
    c = lax.broadcasted_iota(jnp.int32, (PAIR, 2 * PAIR), 1) % PAIR
    tri2 = (r >= c).astype(BF16)
    b_parts = []
    for p in range(n_pairs):
        psl = slice(p * PAIR, (p + 1) * PAIR)
        b_parts.append(_dot(tri2, jnp.concatenate([la_hi[psl], la_lo[psl]], axis=0)))
    v = proj(C_V, GLA_WIDTH)

    y = []
    for g in range(SG_GROUPS):
        v_g = _gelu(cols(sv, g))
        mu = jnp.mean(v_g, axis=-1, keepdims=True)
        d = v_g - mu
        var = jnp.mean(d * d, axis=-1, keepdims=True)
        y.append((d * lax.rsqrt(var + NORM_EPS) * lnw_ref[g:g + 1, :]
                  + lnb_ref[g:g + 1, :]).astype(BF16))

    lane_head = lax.broadcasted_iota(jnp.int32, (PAIR, GLA_KEY), 1) // GLA_DK
    pr = lax.broadcasted_iota(jnp.int32, (GLA_HEADS * PAIR, PAIR), 0) % PAIR
    pc = lax.broadcasted_iota(jnp.int32, (GLA_HEADS * PAIR, PAIR), 1)
    causal = pr >= pc

    def gla_scores(p):
        psl = slice(p * PAIR, (p + 1) * PAIR)
        b = b_parts[p]
        b_mid = b[GLA_CHUNK - 1:GLA_CHUNK]
        b_end = b[PAIR - 1:PAIR]
        qs = q[psl] * (GLA_DK ** -0.5)
        kp = k[psl]
        q_rel = qs * jnp.exp(b - b_mid)
        k_rel_t = (kp * jnp.exp(b_mid - b)).T.astype(BF16)
        q_in = qs * jnp.exp(b)
        k_out_t = (kp * jnp.exp(b_end - b)).T.astype(BF16)
        decay_t = jnp.exp(jnp.broadcast_to(b_end, (LANES, GLA_KEY)).T)
        scores = []
        for g in range(GLA_HEADS // HEADS_PER_GROUP):
            lanes = slice(g * LANES, (g + 1) * LANES)
            q_stack = jnp.concatenate(
                [jnp.where(lane_head == h, q_rel, 0.0)[:, lanes]
                 for h in range(g * HEADS_PER_GROUP, (g + 1) * HEADS_PER_GROUP)], axis=0)
            scores.append(_dot(q_stack.astype(BF16), k_rel_t[lanes]))
        probs = jnp.where(causal, jnp.concatenate(scores, axis=0), 0.0).astype(BF16)
        q_heads = [jnp.where(lane_head == h, q_in, 0.0)[:, (h // HEADS_PER_GROUP) * LANES:
                                                         (h // HEADS_PER_GROUP + 1) * LANES].astype(BF16)
                   for h in range(GLA_HEADS)]
        return probs, q_heads, k_out_t, decay_t

    def gla_apply(p, st, probs, q_heads, k_out_t, decay_t):
        vp_b = v[p * PAIR:(p + 1) * PAIR].astype(BF16)
        st_b = st.astype(BF16)
        zeros = jnp.zeros((GLA_DK, LANES), BF16)
        o, u = [], []
        for h in range(GLA_HEADS):
            g = h // HEADS_PER_GROUP
            v_h = vp_b[:, h * GLA_DV:(h + 1) * GLA_DV]
            lhs = jnp.concatenate(
                [jnp.concatenate([probs[h * PAIR:(h + 1) * PAIR], q_heads[h]], axis=1),
                 jnp.concatenate([k_out_t[h * GLA_DK:(h + 1) * GLA_DK], zeros], axis=1)], axis=0)
            rhs = jnp.concatenate([v_h, st_b[g * LANES:(g + 1) * LANES]], axis=0)
            ou = _dot(lhs, rhs)
            o.append(ou[:PAIR])
            u.append(ou[PAIR:])
        ogla_ref[base + p * PAIR:base + (p + 1) * PAIR, :] = jnp.concatenate(o, axis=1)
        return st * decay_t + jnp.concatenate(u, axis=0)

    wr = lax.broadcasted_iota(jnp.int32, (SG_CHUNK, SG_CHUNK), 0)
    wc = lax.broadcasted_iota(jnp.int32, (SG_CHUNK, SG_CHUNK), 1)

    def sg_mix(su):
        for g in range(SG_GROUPS):
            u_g = _gelu(cols(su, g))
            w_g = jnp.where(wr >= wc, ws_ref[g], jnp.zeros((), BF16))
            bias = bst_ref[:, g:g + 1]
            for p in range(tt // SG_CHUNK):
                rs = slice(p * SG_CHUNK, (p + 1) * SG_CHUNK)
                mixed = _dot(w_g, y[g][rs]) + bias
                mix_ref[base + p * SG_CHUNK:base + (p + 1) * SG_CHUNK,
                        GLA_WIDTH + g * SG_CH:GLA_WIDTH + (g + 1) * SG_CH] = (
                            u_g[rs] * mixed).astype(BF16)

    su = [proj_wide(C_SU, j) for j in range(SG_WIDTH // PROJ_COLS)]
    go = []
    acc = []
    fills = [lambda: sg_mix(su),
             lambda: go.append(proj_wide(C_GOUT, 0)),
             lambda: go.append(proj_wide(C_GOUT, 1)),
             lambda: acc.append(x + _dot(mix_ref[rows, GLA_WIDTH:], wout_ref[GLA_WIDTH:, :]))]
    assert len(fills) == n_pairs
    staged = gla_scores(0)
    for p in range(n_pairs):
        fills[p]()
        nxt = gla_scores(p + 1) if p + 1 < n_pairs else None
        st = gla_apply(p, st, *staged)
        staged = nxt

    acc = acc[0]
    for h in range(GLA_HEADS):
        hs = slice(h * GLA_DV, (h + 1) * GLA_DV)
        o_h = _rms(ogla_ref[rows, hs], gnw_ref[...])
        g_h = cols(go, h)
        mix_ref[rows, hs] = (o_h * (g_h * jax.nn.sigmoid(g_h))).astype(BF16)
    o_ref[0, rows, :] = acc + _dot(mix_ref[rows, :GLA_WIDTH], wout_ref[:GLA_WIDTH, :])
    return st


def _ffn_kernel(h_ref, n2w_ref, wg_ref, wu_ref, wd_ref, fnw_ref, o_ref, hid_ref):
    n_sub = h_ref.shape[0] // FFN_SUB
    n_chunks = D_FF // FF_CHUNK
    subs = [slice(s * FFN_SUB, (s + 1) * FFN_SUB) for s in range(n_sub)]
    xw = [None] * n_sub
    inv = [None] * n_sub

    def gate_up(s, f):
        if xw[s] is None:
            h = h_ref[subs[s], :]
            xw[s] = (h * n2w_ref[...]).astype(BF16)
            inv[s] = lax.rsqrt(jnp.mean(h * h, axis=-1, keepdims=True) + NORM_EPS)
        fs = slice(f * FF_CHUNK, (f + 1) * FF_CHUNK)
        a = _dot(xw[s], wg_ref[:, fs]) * inv[s]
        u = _dot(xw[s], wu_ref[:, fs]) * inv[s]
        hid_ref[subs[s], fs] = (a * jax.nn.sigmoid(a) * u).astype(BF16)

    def down(s):
        piece = FFN_SUB // DOWN_SPLIT
        for j in range(DOWN_SPLIT):
            rows = slice(subs[s].start + j * piece, subs[s].start + (j + 1) * piece)
            h2 = h_ref[rows, :] + _dot(hid_ref[rows, :], wd_ref[...])
            o_ref[rows, :] = _rms(h2, fnw_ref[...])

    for s in range(n_sub):
        for f in range(FFN_LEAD if s else 0, n_chunks):
            gate_up(s, f)
        if s + 1 < n_sub:
            for f in range(FFN_LEAD):
                gate_up(s + 1, f)
        down(s)


def _const_spec(shape):
    nd = len(shape)
    return pl.BlockSpec(shape, lambda *_: (0,) * nd, pipeline_mode=pl.Buffered(1))


def _mixer(x, n1w, wa, wb, wgu, bgu, gnw, lnw, lnb, ws, bst, wout, ffn_ws):
    bsz, seq, _ = x.shape
    assert seq % MIX_TILE == 0 and MIX_TILE % MIX_SUB == 0 and MIX_SUB % PAIR == 0
    n_t = seq // MIX_TILE
    consts = (n1w, wa, wb, wgu, bgu, gnw, lnw, lnb, ws, bst, wout)

    def slab_spec(w):
        rows, width = w.shape
        slab = next(r for r in range(BF16_SUBLANES, rows + 1, BF16_SUBLANES)
                    if rows % r == 0 and r * bsz * n_t >= rows)
        last = rows // slab - 1
        return pl.BlockSpec((slab, width), lambda b, t: (jnp.minimum(b * n_t + t, last), 0))

    slab_specs = [slab_spec(w) for w in ffn_ws]
    tile_spec = pl.BlockSpec((1, MIX_TILE, D_MODEL), lambda b, t: (b, t, 0))
    h, *narrowed = pl.pallas_call(
        _mixer_kernel,
        grid=(bsz, n_t),
        in_specs=[tile_spec] + [_const_spec(a.shape) for a in consts] + slab_specs,
        out_specs=[tile_spec] + slab_specs,
        out_shape=[jax.ShapeDtypeStruct(x.shape, F32)]
        + [jax.ShapeDtypeStruct(w.shape, BF16) for w in ffn_ws],
        scratch_shapes=[
            pltpu.VMEM((GLA_KEY, GLA_DV), F32),
            pltpu.VMEM((MIX_TILE, GLA_WIDTH), F32),
            pltpu.VMEM((MIX_TILE, D_MODEL), BF16),
            pltpu.VMEM((D_MODEL, GLA_KEY), BF16),
        ],
        compiler_params=pltpu.CompilerParams(
            dimension_semantics=("arbitrary", "arbitrary"),
            vmem_limit_bytes=VMEM_LIMIT),
        name="mixer",
    )(x, *consts, *ffn_ws)
    return h, narrowed


def _ffn(h, n2w, wg, wu, wd, fnw):
    m = h.shape[0]
    assert m % FFN_TILE == 0
    consts = (n2w, wg, wu, wd, fnw)
    return pl.pallas_call(
        _ffn_kernel,
        grid=(m // FFN_TILE,),
        in_specs=[pl.BlockSpec((FFN_TILE, D_MODEL), lambda i: (i, 0))]
        + [_const_spec(a.shape) for a in consts],
        out_specs=pl.BlockSpec((FFN_TILE, D_MODEL), lambda i: (i, 0)),
        out_shape=jax.ShapeDtypeStruct(h.shape, F32),
        scratch_shapes=[pltpu.VMEM((FFN_TILE, D_FF), BF16)],
        compiler_params=pltpu.CompilerParams(
            dimension_semantics=("parallel",),
            vmem_limit_bytes=VMEM_LIMIT),
        name="ffn",
    )(h, *consts)


def kernel(x, norm1_w, w_in, w_gate_up, b_gate_up, gla_norm_w, sg_ln_w, sg_ln_b, sg_w_s, sg_b_s,
           w_out, norm2_w, w_ffn_gate, w_ffn_up, w_ffn_down, final_norm_w):
    bsz, seq, _ = x.shape
    assert w_in.shape[0] == 1, "the final norm is fused into the (single) layer's ffn call"
    wi = w_in[0]
    o_glr = GLA_KEY + GLA_KEY + GLA_WIDTH + GATE_RANK
    wa = jnp.pad(wi[:, :o_glr], ((0, 0), (0, GLR_PAD - GATE_RANK))).astype(BF16)
    wb = wi[:, o_glr:].astype(BF16)
    wgu = jnp.concatenate(
        [w_gate_up[0], jnp.zeros((GLR_PAD - GATE_RANK, GLA_KEY), w_gate_up.dtype)],
        axis=0).astype(BF16)
    h, (wg, wu, wd) = _mixer(
        x, norm1_w[0][None, :], wa, wb, wgu, b_gate_up[0][None, :], gla_norm_w[0][None, :],
        sg_ln_w[0], sg_ln_b[0], sg_w_s[0].astype(BF16), jnp.transpose(sg_b_s[0]),
        w_out[0].astype(BF16), (w_ffn_gate[0], w_ffn_up[0], w_ffn_down[0]))
    out = _ffn(h.reshape(bsz * seq, D_MODEL), norm2_w[0][None, :], wg, wu, wd,
               final_norm_w[None, :])
    return out.reshape(bsz, seq, D_MODEL)
```

```python
import jax
import jax.numpy as jnp
from jax import lax
from jax.experimental import pallas as pl
from jax.experimental.pallas import tpu as pltpu

F32 = jnp.float32
BF16 = jnp.bfloat16

D_MODEL = 1024
GLA_HEADS = 4
GLA_DK = 64
GLA_DV = 128
GLA_KEY = GLA_HEADS * GLA_DK
GLA_WIDTH = GLA_HEADS * GLA_DV
GATE_RANK = 16
GATE_NORMALIZER = 16.0
GLA_CHUNK = 64
SG_GROUPS = 4
SG_CH = 128
SG_WIDTH = SG_GROUPS * SG_CH
SG_CHUNK = 128
D_FF = 2816
NORM_EPS = 1e-5

LANES = 128
BF16_SUBLANES = 16
GLR_PAD = LANES

C_Q = 0
C_K = C_Q + GLA_KEY
C_V = C_K + GLA_KEY
C_GLR = C_V + GLA_WIDTH
A_COLS = C_GLR + GLR_PAD
C_GOUT = A_COLS
C_SU = C_GOUT + GLA_WIDTH
C_SV = C_SU + SG_WIDTH

PAIR = 2 * GLA_CHUNK
HEADS_PER_GROUP = LANES // GLA_DK
PROJ_COLS = 256

MIX_TILE = 2048
MIX_SUB = 512
FFN_TILE = 1024
FFN_SUB = 512
FF_CHUNK = 256
DOWN_SPLIT = 2
FFN_LEAD = 1
NARROW_ROWS = 128
VMEM_LIMIT = 56 * 1024 * 1024


def _rms(x, w):
    ms = jnp.mean(x * x, axis=-1, keepdims=True)
    return x * lax.rsqrt(ms + NORM_EPS) * w


def _gelu(x):
    return 0.5 * x * (1.0 + lax.erf(x * (0.5 ** 0.5)))


def _dot(a, b):
    return jnp.dot(a, b, preferred_element_type=F32)


def _mixer_kernel(x_ref, n1w_ref, wa_ref, wb_ref, wgu_ref, bgu_ref, gnw_ref, lnw_ref, lnb_ref,
                  ws_ref, bst_ref, wout_ref, f0_ref, f1_ref, f2_ref,
                  o_ref, f0b_ref, f1b_ref, f2b_ref, st_ref, ogla_ref, mix_ref, wfold_ref):
    @pl.when(pl.program_id(1) == 0)
    def _():
        st_ref[...] = jnp.zeros_like(st_ref)

    @pl.when((pl.program_id(0) == 0) & (pl.program_id(1) == 0))
    def _():
        wfold_ref[...] = _dot(wa_ref[:, C_GLR:C_GLR + GLR_PAD], wgu_ref[...]).astype(BF16)

    st = st_ref[...]
    for s in range(x_ref.shape[1] // MIX_SUB):
        st = _mixer_sub_tile(s * MIX_SUB, st, x_ref, n1w_ref, wa_ref, wb_ref, wfold_ref, bgu_ref,
                             gnw_ref, lnw_ref, lnb_ref, ws_ref, bst_ref, wout_ref, o_ref, ogla_ref,
                             mix_ref)
    st_ref[...] = st

    for src, dst in ((f0_ref, f0b_ref), (f1_ref, f1b_ref), (f2_ref, f2b_ref)):
        dst[...] = src[...].astype(BF16)


def _mixer_sub_tile(base, st, x_ref, n1w_ref, wa_ref, wb_ref, wfold_ref, bgu_ref, gnw_ref, lnw_ref,
                    lnb_ref, ws_ref, bst_ref, wout_ref, o_ref, ogla_ref, mix_ref):
    tt = MIX_SUB
    n_pairs = tt // PAIR
    per = PROJ_COLS // LANES
    rows = slice(base, base + tt)

    x = x_ref[0, rows, :]
    n = _rms(x, n1w_ref[...]).astype(BF16)

    def proj(lo, width):
        if lo < A_COLS:
            return _dot(n, wa_ref[:, lo:lo + width])
        return _dot(n, wb_ref[:, lo - A_COLS:lo - A_COLS + width])

    def proj_wide(lo, j):
        return proj(lo + j * PROJ_COLS, PROJ_COLS)

    def cols(pieces, g):
        return pieces[g // per][:, (g % per) * LANES:(g % per + 1) * LANES]

    gk = _dot(n, wfold_ref[...]) + bgu_ref[...]
    sv = [proj_wide(C_SV, j) for j in range(SG_WIDTH // PROJ_COLS)]
    log_a = (jnp.minimum(gk, 0.0) - jnp.log1p(jnp.exp(-jnp.abs(gk)))) * (1.0 / GATE_NORMALIZER)
    k = proj(C_K, GLA_KEY)
    q = proj(C_Q, GLA_KEY)

    la_hi = log_a.astype(BF16)
    la_lo = (log_a - la_hi.astype(F32)).astype(BF16)
    r = lax.broadcasted_iota(jnp.int32, (PAIR, 2 * PAIR), 0)
    c = lax.broadcasted_iota(jnp.int32, (PAIR, 2 * PAIR), 1) % PAIR
    tri2 = (r >= c).astype(BF16)
    b_parts = []
    for p in range(n_pairs):
        psl = slice(p * PAIR, (p + 1) * PAIR)
        b_parts.append(_dot(tri2, jnp.concatenate([la_hi[psl], la_lo[psl]], axis=0)))
    v = proj(C_V, GLA_WIDTH)

    y = []
    for g in range(SG_GROUPS):
        v_g = _gelu(cols(sv, g))
        mu = jnp.mean(v_g, axis=-1, keepdims=True)
        d = v_g - mu
        var = jnp.mean(d * d, axis=-1, keepdims=True)
        y.append((d * lax.rsqrt(var + NORM_EPS) * lnw_ref[g:g + 1, :]
                  + lnb_ref[g:g + 1, :]).astype(BF16))

    lane_head = lax.broadcasted_iota(jnp.int32, (PAIR, GLA_KEY), 1) // GLA_DK
    pr = lax.broadcasted_iota(jnp.int32, (GLA_HEADS * PAIR, PAIR), 0) % PAIR
    pc = lax.broadcasted_iota(jnp.int32, (GLA_HEADS * PAIR, PAIR), 1)
    causal = pr >= pc

    def gla_scores(p):
        psl = slice(p * PAIR, (p + 1) * PAIR)
        b = b_parts[p]
        b_mid = b[GLA_CHUNK - 1:GLA_CHUNK]
        b_end = b[PAIR - 1:PAIR]
        qs = q[psl] * (GLA_DK ** -0.5)
        kp = k[psl]
        q_rel = qs * jnp.exp(b - b_mid)
        k_rel_t = (kp * jnp.exp(b_mid - b)).T.astype(BF16)
        q_in = qs * jnp.exp(b)
        k_out_t = (kp * jnp.exp(b_end - b)).T.astype(BF16)
        decay_t = jnp.exp(jnp.broadcast_to(b_end, (LANES, GLA_KEY)).T)
        scores = []
        for g in range(GLA_HEADS // HEADS_PER_GROUP):
            lanes = slice(g * LANES, (g + 1) * LANES)
            q_stack = jnp.concatenate(
                [jnp.where(lane_head == h, q_rel, 0.0)[:, lanes]
                 for h in range(g * HEADS_PER_GROUP, (g + 1) * HEADS_PER_GROUP)], axis=0)
            scores.append(_dot(q_stack.astype(BF16), k_rel_t[lanes]))
        probs = jnp.where(causal, jnp.concatenate(scores, axis=0), 0.0).astype(BF16)
        q_heads = [jnp.where(lane_head == h, q_in, 0.0)[:, (h // HEADS_PER_GROUP) * LANES:
                                                         (h // HEADS_PER_GROUP + 1) * LANES].astype(BF16)
                   for h in range(GLA_HEADS)]
        return probs, q_heads, k_out_t, decay_t

    def gla_apply(p, st, probs, q_heads, k_out_t, decay_t):
        vp_b = v[p * PAIR:(p + 1) * PAIR].astype(BF16)
        st_b = st.astype(BF16)
        zeros = jnp.zeros((GLA_DK, LANES), BF16)
        o, u = [], []
        for h in range(GLA_HEADS):
            g = h // HEADS_PER_GROUP
            v_h = vp_b[:, h * GLA_DV:(h + 1) * GLA_DV]
            lhs = jnp.concatenate(
                [jnp.concatenate([probs[h * PAIR:(h + 1) * PAIR], q_heads[h]], axis=1),
                 jnp.concatenate([k_out_t[h * GLA_DK:(h + 1) * GLA_DK], zeros], axis=1)], axis=0)
            rhs = jnp.concatenate([v_h, st_b[g * LANES:(g + 1) * LANES]], axis=0)
            ou = _dot(lhs, rhs)
            o.append(ou[:PAIR])
            u.append(ou[PAIR:])
        ogla_ref[base + p * PAIR:base + (p + 1) * PAIR, :] = jnp.concatenate(o, axis=1)
        return st * decay_t + jnp.concatenate(u, axis=0)

    wr = lax.broadcasted_iota(jnp.int32, (SG_CHUNK, SG_CHUNK), 0)
    wc = lax.broadcasted_iota(jnp.int32, (SG_CHUNK, SG_CHUNK), 1)

    def sg_mix(su):
        for g in range(SG_GROUPS):
            u_g = _gelu(cols(su, g))
            w_g = jnp.where(wr >= wc, ws_ref[g], jnp.zeros((), BF16))
            bias = bst_ref[:, g:g + 1]
            for p in range(tt // SG_CHUNK):
                rs = slice(p * SG_CHUNK, (p + 1) * SG_CHUNK)
                mixed = _dot(w_g, y[g][rs]) + bias
                mix_ref[base + p * SG_CHUNK:base + (p + 1) * SG_CHUNK,
                        GLA_WIDTH + g * SG_CH:GLA_WIDTH + (g + 1) * SG_CH] = (
                            u_g[rs] * mixed).astype(BF16)

    su = [proj_wide(C_SU, j) for j in range(SG_WIDTH // PROJ_COLS)]
    go = []
    acc = []
    fills = [lambda: sg_mix(su),
             lambda: go.append(proj_wide(C_GOUT, 0)),
             lambda: go.append(proj_wide(C_GOUT, 1)),
             lambda: acc.append(x + _dot(mix_ref[rows, GLA_WIDTH:], wout_ref[GLA_WIDTH:, :]))]
    assert len(fills) == n_pairs
    staged = gla_scores(0)
    for p in range(n_pairs):
        fills[p]()
        nxt = gla_scores(p + 1) if p + 1 < n_pairs else None
        st = gla_apply(p, st, *staged)
        staged = nxt

    acc = acc[0]
    for h in range(GLA_HEADS):
        hs = slice(h * GLA_DV, (h + 1) * GLA_DV)
        o_h = _rms(ogla_ref[rows, hs], gnw_ref[...])
        g_h = cols(go, h)
        mix_ref[rows, hs] = (o_h * (g_h * jax.nn.sigmoid(g_h))).astype(BF16)
    o_ref[0, rows, :] = acc + _dot(mix_ref[rows, :GLA_WIDTH], wout_ref[:GLA_WIDTH, :])
    return st


def _ffn_kernel(h_ref, n2w_ref, wg_ref, wu_ref, wd_ref, fnw_ref, o_ref, hid_ref):
    n_sub = h_ref.shape[0] // FFN_SUB
    n_chunks = D_FF // FF_CHUNK
    subs = [slice(s * FFN_SUB, (s + 1) * FFN_SUB) for s in range(n_sub)]
    xw = [None] * n_sub
    inv = [None] * n_sub

    def gate_up(s, f):
        if xw[s] is None:
            h = h_ref[subs[s], :]
            xw[s] = (h * n2w_ref[...]).astype(BF16)
            inv[s] = lax.rsqrt(jnp.mean(h * h, axis=-1, keepdims=True) + NORM_EPS)
        fs = slice(f * FF_CHUNK, (f + 1) * FF_CHUNK)
        a = _dot(xw[s], wg_ref[:, fs]) * inv[s]
        u = _dot(xw[s], wu_ref[:, fs]) * inv[s]
        hid_ref[subs[s], fs] = (a * jax.nn.sigmoid(a) * u).astype(BF16)

    def down(s):
        piece = FFN_SUB // DOWN_SPLIT
        for j in range(DOWN_SPLIT):
            rows = slice(subs[s].start + j * piece, subs[s].start + (j + 1) * piece)
            h2 = h_ref[rows, :] + _dot(hid_ref[rows, :], wd_ref[...])
            o_ref[rows, :] = _rms(h2, fnw_ref[...])

    for s in range(n_sub):
        for f in range(FFN_LEAD if s else 0, n_chunks):
            gate_up(s, f)
        if s + 1 < n_sub:
            for f in range(FFN_LEAD):
                gate_up(s + 1, f)
        down(s)


def _const_spec(shape):
    nd = len(shape)
    return pl.BlockSpec(shape, lambda *_: (0,) * nd, pipeline_mode=pl.Buffered(1))


def _narrow_kernel(wi_ref, wo_ref, wa_ref, wb_ref, wob_ref):
    qkv = C_GLR
    wa_ref[:, :qkv] = wi_ref[:, :qkv].astype(BF16)
    lane = lax.broadcasted_iota(jnp.int32, (wi_ref.shape[0], GLR_PAD), 1)
    wa_ref[:, qkv:] = jnp.where(lane < GATE_RANK, wi_ref[:, qkv:qkv + GLR_PAD], 0.0).astype(BF16)
    wb_ref[...] = wi_ref[:, qkv + GATE_RANK:].astype(BF16)
    wob_ref[...] = wo_ref[...].astype(BF16)


def _narrow_mixer_weights(wi, wo):
    rows, in_cols = wi.shape
    b_cols = in_cols - C_GLR - GATE_RANK
    blk = NARROW_ROWS
    assert rows % blk == 0 and wo.shape[0] == rows
    return pl.pallas_call(
        _narrow_kernel,
        grid=(rows // blk,),
        in_specs=[pl.BlockSpec((blk, in_cols), lambda i: (i, 0)),
                  pl.BlockSpec((blk, wo.shape[1]), lambda i: (i, 0))],
        out_specs=[pl.BlockSpec((blk, A_COLS), lambda i: (i, 0)),
                   pl.BlockSpec((blk, b_cols), lambda i: (i, 0)),
                   pl.BlockSpec((blk, wo.shape[1]), lambda i: (i, 0))],
        out_shape=[jax.ShapeDtypeStruct((rows, A_COLS), BF16),
                   jax.ShapeDtypeStruct((rows, b_cols), BF16),
                   jax.ShapeDtypeStruct(wo.shape, BF16)],
        compiler_params=pltpu.CompilerParams(dimension_semantics=("parallel",)),
        name="narrow",
    )(wi, wo)


def _mixer(x, n1w, wa, wb, wgu, bgu, gnw, lnw, lnb, ws, bst, wout, ffn_ws):
    bsz, seq, _ = x.shape
    assert seq % MIX_TILE == 0 and MIX_TILE % MIX_SUB == 0 and MIX_SUB % PAIR == 0
    n_t = seq // MIX_TILE
    consts = (n1w, wa, wb, wgu, bgu, gnw, lnw, lnb, ws, bst, wout)

    def slab_spec(w):
        rows, width = w.shape
        slab = next(r for r in range(BF16_SUBLANES, rows + 1, BF16_SUBLANES)
                    if rows % r == 0 and r * bsz * n_t >= rows)
        last = rows // slab - 1
        return pl.BlockSpec((slab, width), lambda b, t: (jnp.minimum(b * n_t + t, last), 0))

    slab_specs = [slab_spec(w) for w in ffn_ws]
    tile_spec = pl.BlockSpec((1, MIX_TILE, D_MODEL), lambda b, t: (b, t, 0))
    h, *narrowed = pl.pallas_call(
        _mixer_kernel,
        grid=(bsz, n_t),
        in_specs=[tile_spec] + [_const_spec(a.shape) for a in consts] + slab_specs,
        out_specs=[tile_spec] + slab_specs,
        out_shape=[jax.ShapeDtypeStruct(x.shape, F32)]
        + [jax.ShapeDtypeStruct(w.shape, BF16) for w in ffn_ws],
        scratch_shapes=[
            pltpu.VMEM((GLA_KEY, GLA_DV), F32),
            pltpu.VMEM((MIX_TILE, GLA_WIDTH), F32),
            pltpu.VMEM((MIX_TILE, D_MODEL), BF16),
            pltpu.VMEM((D_MODEL, GLA_KEY), BF16),
        ],
        compiler_params=pltpu.CompilerParams(
            dimension_semantics=("arbitrary", "arbitrary"),
            vmem_limit_bytes=VMEM_LIMIT),
        name="mixer",
    )(x, *consts, *ffn_ws)
    return h, narrowed


def _ffn(h, n2w, wg, wu, wd, fnw):
    m = h.shape[0]
    assert m % FFN_TILE == 0
    consts = (n2w, wg, wu, wd, fnw)
    return pl.pallas_call(
        _ffn_kernel,
        grid=(m // FFN_TILE,),
        in_specs=[pl.BlockSpec((FFN_TILE, D_MODEL), lambda i: (i, 0))]
        + [_const_spec(a.shape) for a in consts],
        out_specs=pl.BlockSpec((FFN_TILE, D_MODEL), lambda i: (i, 0)),
        out_shape=jax.ShapeDtypeStruct(h.shape, F32),
        scratch_shapes=[pltpu.VMEM((FFN_TILE, D_FF), BF16)],
        compiler_params=pltpu.CompilerParams(
            dimension_semantics=("parallel",),
            vmem_limit_bytes=VMEM_LIMIT),
        name="ffn",
    )(h, *consts)


def kernel(x, norm1_w, w_in, w_gate_up, b_gate_up, gla_norm_w, sg_ln_w, sg_ln_b, sg_w_s, sg_b_s,
           w_out, norm2_w, w_ffn_gate, w_ffn_up, w_ffn_down, final_norm_w):
    bsz, seq, _ = x.shape
    assert w_in.shape[0] == 1, "the final norm is fused into the (single) layer's ffn call"
    wa, wb, wout = _narrow_mixer_weights(w_in[0], w_out[0])
    wgu = jnp.concatenate(
        [w_gate_up[0], jnp.zeros((GLR_PAD - GATE_RANK, GLA_KEY), w_gate_up.dtype)],
        axis=0).astype(BF16)
    h, (wg, wu, wd) = _mixer(
        x, norm1_w[0][None, :], wa, wb, wgu, b_gate_up[0][None, :], gla_norm_w[0][None, :],
        sg_ln_w[0], sg_ln_b[0], sg_w_s[0].astype(BF16), jnp.transpose(sg_b_s[0]),
        wout, (w_ffn_gate[0], w_ffn_up[0], w_ffn_down[0]))
    out = _ffn(h.reshape(bsz * seq, D_MODEL), norm2_w[0][None, :], wg, wu, wd,
               final_norm_w[None, :])
    return out.reshape(bsz, seq, D_MODEL)
```

```python
import jax
import jax.numpy as jnp
from jax import lax
from jax.experimental import pallas as pl
from jax.experimental.pallas import tpu as pltpu

F32 = jnp.float32
BF16 = jnp.bfloat16

D_MODEL = 1024
GLA_HEADS = 4
GLA_DK = 64
GLA_DV = 128
GLA_KEY = GLA_HEADS * GLA_DK
GLA_WIDTH = GLA_HEADS * GLA_DV
GATE_RANK = 16
GATE_NORMALIZER = 16.0
GLA_CHUNK = 64
SG_GROUPS = 4
SG_CH = 128
SG_WIDTH = SG_GROUPS * SG_CH
SG_CHUNK = 128
D_FF = 2816
NORM_EPS = 1e-5

LANES = 128
BF16_SUBLANES = 16
GLR_PAD = LANES

C_Q = 0
C_K = C_Q + GLA_KEY
C_V = C_K + GLA_KEY
C_GLR = C_V + GLA_WIDTH
A_COLS = C_GLR + GLR_PAD
C_GOUT = A_COLS
C_SU = C_GOUT + GLA_WIDTH
C_SV = C_SU + SG_WIDTH

PAIR = 2 * GLA_CHUNK
HEADS_PER_GROUP = LANES // GLA_DK
PROJ_COLS = 256

MIX_TILE = 2048
MIX_SUB = 512
FFN_TILE = 1024
FFN_SUB = 512
FF_CHUNK = 256
DOWN_SPLIT = 2
FFN_LEAD = 1
VMEM_LIMIT = 56 * 1024 * 1024


def _rms(x, w):
    ms = jnp.mean(x * x, axis=-1, keepdims=True)
    return x * lax.rsqrt(ms + NORM_EPS) * w


def _gelu(x):
    return 0.5 * x * (1.0 + lax.erf(x * (0.5 ** 0.5)))


def _dot(a, b):
    return jnp.dot(a, b, preferred_element_type=F32)


def _mixer_kernel(x_ref, n1w_ref, wa_ref, wb_ref, wgu_ref, bgu_ref, gnw_ref, lnw_ref, lnb_ref,
                  ws_ref, bst_ref, wout_ref, f0_ref, f1_ref, f2_ref,
                  o_ref, f0b_ref, f1b_ref, f2b_ref, st_ref, ogla_ref, mix_ref, wfold_ref):
    @pl.when(pl.program_id(1) == 0)
    def _():
        st_ref[...] = jnp.zeros_like(st_ref)

    @pl.when((pl.program_id(0) == 0) & (pl.program_id(1) == 0))
    def _():
        wfold_ref[...] = _dot(wa_ref[:, C_GLR:C_GLR + GLR_PAD], wgu_ref[...]).astype(BF16)

    st = st_ref[...]
    for s in range(x_ref.shape[1] // MIX_SUB):
        st = _mixer_sub_tile(s * MIX_SUB, st, x_ref, n1w_ref, wa_ref, wb_ref, wfold_ref, bgu_ref,
                             gnw_ref, lnw_ref, lnb_ref, ws_ref, bst_ref, wout_ref, o_ref, ogla_ref,
                             mix_ref)
    st_ref[...] = st

    for src, dst in ((f0_ref, f0b_ref), (f1_ref, f1b_ref), (f2_ref, f2b_ref)):
        dst[...] = src[...].astype(BF16)


def _mixer_sub_tile(base, st, x_ref, n1w_ref, wa_ref, wb_ref, wfold_ref, bgu_ref, gnw_ref, lnw_ref,
                    lnb_ref, ws_ref, bst_ref, wout_ref, o_ref, ogla_ref, mix_ref):
    tt = MIX_SUB
    n_pairs = tt // PAIR
    per = PROJ_COLS // LANES
    rows = slice(base, base + tt)

    x = x_ref[0, rows, :]
    n = _rms(x, n1w_ref[...]).astype(BF16)

    def proj(lo, width):
        if lo < A_COLS:
            return _dot(n, wa_ref[:, lo:lo + width])
        return _dot(n, wb_ref[:, lo - A_COLS:lo - A_COLS + width])

    def proj_wide(lo, j):
        return proj(lo + j * PROJ_COLS, PROJ_COLS)

    def cols(pieces, g):
        return pieces[g // per][:, (g % per) * LANES:(g % per + 1) * LANES]

    gk = _dot(n, wfold_ref[...]) + bgu_ref[...]
    sv = [proj_wide(C_SV, j) for j in range(SG_WIDTH // PROJ_COLS)]
    log_a = (jnp.minimum(gk, 0.0) - jnp.log1p(jnp.exp(-jnp.abs(gk)))) * (1.0 / GATE_NORMALIZER)
    k = proj(C_K, GLA_KEY)
    q = proj(C_Q, GLA_KEY)

    la_hi = log_a.astype(BF16)
    la_lo = (log_a - la_hi.astype(F32)).astype(BF16)
    r = lax.broadcasted_iota(jnp.int32, (PAIR, 2 * PAIR), 0)
    c = lax.broadcasted_iota(jnp.int32, (PAIR, 2 * PAIR), 1) % PAIR
    tri2 = (r >= c).astype(BF16)
    b_parts = []
    for p in range(n_pairs):
        psl = slice(p * PAIR, (p + 1) * PAIR)
        b_parts.append(_dot(tri2, jnp.concatenate([la_hi[psl], la_lo[psl]], axis=0)))
    v = proj(C_V, GLA_WIDTH)

    y = []
    for g in range(SG_GROUPS):
        v_g = _gelu(cols(sv, g))
        mu = jnp.mean(v_g, axis=-1, keepdims=True)
        d = v_g - mu
        var = jnp.mean(d * d, axis=-1, keepdims=True)
        y.append((d * lax.rsqrt(var + NORM_EPS) * lnw_ref[g:g + 1, :]
                  + lnb_ref[g:g + 1, :]).astype(BF16))

    lane_head = lax.broadcasted_iota(jnp.int32, (PAIR, GLA_KEY), 1) // GLA_DK
    pr = lax.broadcasted_iota(jnp.int32, (GLA_HEADS * PAIR, PAIR), 0) % PAIR
    pc = lax.broadcasted_iota(jnp.int32, (GLA_HEADS * PAIR, PAIR), 1)
    causal = pr >= pc

    def gla_scores(p):
        psl = slice(p * PAIR, (p + 1) * PAIR)
        b = b_parts[p]
        b_mid = b[GLA_CHUNK - 1:GLA_CHUNK]
        b_end = b[PAIR - 1:PAIR]
        qs = q[psl] * (GLA_DK ** -0.5)
        kp = k[psl]
        q_rel = qs * jnp.exp(b - b_mid)
        k_rel_t = (kp * jnp.exp(b_mid - b)).T.astype(BF16)
        q_in = qs * jnp.exp(b)
        k_out_t = (kp * jnp.exp(b_end - b)).T.astype(BF16)
        decay_t = jnp.exp(jnp.broadcast_to(b_end, (LANES, GLA_KEY)).T)
        scores = []
        for g in range(GLA_HEADS // HEADS_PER_GROUP):
            lanes = slice(g * LANES, (g + 1) * LANES)
            q_stack = jnp.concatenate(
                [jnp.where(lane_head == h, q_rel, 0.0)[:, lanes]
                 for h in range(g * HEADS_PER_GROUP, (g + 1) * HEADS_PER_GROUP)], axis=0)
            scores.append(_dot(q_stack.astype(BF16), k_rel_t[lanes]))
        probs = jnp.where(causal, jnp.concatenate(scores, axis=0), 0.0).astype(BF16)
        q_heads = [jnp.where(lane_head == h, q_in, 0.0)[:, (h // HEADS_PER_GROUP) * LANES:
                                                         (h // HEADS_PER_GROUP + 1) * LANES].astype(BF16)
                   for h in range(GLA_HEADS)]
        return probs, q_heads, k_out_t, decay_t

    def gla_apply(p, st, probs, q_heads, k_out_t, decay_t):
        vp_b = v[p * PAIR:(p + 1) * PAIR].astype(BF16)
        st_b = st.astype(BF16)
        zeros = jnp.zeros((GLA_DK, LANES), BF16)
        o, u = [], []
        for h in range(GLA_HEADS):
            g = h // HEADS_PER_GROUP
            v_h = vp_b[:, h * GLA_DV:(h + 1) * GLA_DV]
            lhs = jnp.concatenate(
                [jnp.concatenate([probs[h * PAIR:(h + 1) * PAIR], q_heads[h]], axis=1),
                 jnp.concatenate([k_out_t[h * GLA_DK:(h + 1) * GLA_DK], zeros], axis=1)], axis=0)
            rhs = jnp.concatenate([v_h, st_b[g * LANES:(g + 1) * LANES]], axis=0)
            ou = _dot(lhs, rhs)
            o.append(ou[:PAIR])
            u.append(ou[PAIR:])
        ogla_ref[base + p * PAIR:base + (p + 1) * PAIR, :] = jnp.concatenate(o, axis=1)
        return st * decay_t + jnp.concatenate(u, axis=0)

    wr = lax.broadcasted_iota(jnp.int32, (SG_CHUNK, SG_CHUNK), 0)
    wc = lax.broadcasted_iota(jnp.int32, (SG_CHUNK, SG_CHUNK), 1)

    def sg_mix(su):
        for g in range(SG_GROUPS):
            u_g = _gelu(cols(su, g))
            w_g = jnp.where(wr >= wc, ws_ref[g], jnp.zeros((), BF16))
            bias = bst_ref[:, g:g + 1]
            for p in range(tt // SG_CHUNK):
                rs = slice(p * SG_CHUNK, (p + 1) * SG_CHUNK)
                mixed = _dot(w_g, y[g][rs]) + bias
                mix_ref[base + p * SG_CHUNK:base + (p + 1) * SG_CHUNK,
                        GLA_WIDTH + g * SG_CH:GLA_WIDTH + (g + 1) * SG_CH] = (
                            u_g[rs] * mixed).astype(BF16)

    su = [proj_wide(C_SU, j) for j in range(SG_WIDTH // PROJ_COLS)]
    go = []
    acc = []
    fills = [lambda: sg_mix(su),
             lambda: go.append(proj_wide(C_GOUT, 0)),
             lambda: go.append(proj_wide(C_GOUT, 1)),
             lambda: acc.append(x + _dot(mix_ref[rows, GLA_WIDTH:], wout_ref[GLA_WIDTH:, :]))]
    assert len(fills) == n_pairs
    staged = gla_scores(0)
    for p in range(n_pairs):
        fills[p]()
        nxt = gla_scores(p + 1) if p + 1 < n_pairs else None
        st = gla_apply(p, st, *staged)
        staged = nxt

    acc = acc[0]
    for h in range(GLA_HEADS):
        hs = slice(h * GLA_DV, (h + 1) * GLA_DV)
        o_h = _rms(ogla_ref[rows, hs], gnw_ref[...])
        g_h = cols(go, h)
        mix_ref[rows, hs] = (o_h * (g_h * jax.nn.sigmoid(g_h))).astype(BF16)
    o_ref[0, rows, :] = acc + _dot(mix_ref[rows, :GLA_WIDTH], wout_ref[:GLA_WIDTH, :])
    return st


def _ffn_kernel(h_ref, n2w_ref, wg_ref, wu_ref, wd_ref, fnw_ref, o_ref, hid_ref):
    n_sub = h_ref.shape[0] // FFN_SUB
    n_chunks = D_FF // FF_CHUNK
    subs = [slice(s * FFN_SUB, (s + 1) * FFN_SUB) for s in range(n_sub)]
    xw = [None] * n_sub
    inv = [None] * n_sub

    def gate_up(s, f):
        if xw[s] is None:
            h = h_ref[subs[s], :]
            xw[s] = (h * n2w_ref[...]).astype(BF16)
            inv[s] = lax.rsqrt(jnp.mean(h * h, axis=-1, keepdims=True) + NORM_EPS)
        fs = slice(f * FF_CHUNK, (f + 1) * FF_CHUNK)
        a = _dot(xw[s], wg_ref[:, fs]) * inv[s]
        u = _dot(xw[s], wu_ref[:, fs]) * inv[s]
        hid_ref[subs[s], fs] = (a * jax.nn.sigmoid(a) * u).astype(BF16)

    def down(s):
        piece = FFN_SUB // DOWN_SPLIT
        for j in range(DOWN_SPLIT):
            rows = slice(subs[s].start + j * piece, subs[s].start + (j + 1) * piece)
            h2 = h_ref[rows, :] + _dot(hid_ref[rows, :], wd_ref[...])
            o_ref[rows, :] = _rms(h2, fnw_ref[...])

    for s in range(n_sub):
        for f in range(FFN_LEAD if s else 0, n_chunks):
            gate_up(s, f)
        if s + 1 < n_sub:
            for f in range(FFN_LEAD):
                gate_up(s + 1, f)
        down(s)


def _const_spec(shape):
    nd = len(shape)
    return pl.BlockSpec(shape, lambda *_: (0,) * nd, pipeline_mode=pl.Buffered(1))


def _narrow_kernel(wit_ref, wo_ref, wa_ref, wb_ref, wob_ref):
    def put(dst_ref, j, block):
        dst_ref[:, j * LANES:(j + 1) * LANES] = block.T.astype(BF16)

    for j in range(C_GLR // LANES):
        put(wa_ref, j, wit_ref[j * LANES:(j + 1) * LANES, :])
    row = lax.broadcasted_iota(jnp.int32, (GLR_PAD, wit_ref.shape[1]), 0)
    put(wa_ref, C_GLR // LANES,
        jnp.where(row < GATE_RANK, wit_ref[C_GLR:C_GLR + GLR_PAD, :], 0.0))
    b0 = C_GLR + GATE_RANK
    for j in range(wb_ref.shape[1] // LANES):
        put(wb_ref, j, wit_ref[b0 + j * LANES:b0 + (j + 1) * LANES, :])
    wob_ref[...] = wo_ref[...].astype(BF16)


def _narrow_mixer_weights(wi, wo):
    rows, in_cols = wi.shape
    b_cols = in_cols - C_GLR - GATE_RANK
    assert b_cols % LANES == 0 and C_GLR % LANES == 0
    return pl.pallas_call(
        _narrow_kernel,
        out_shape=[jax.ShapeDtypeStruct((rows, A_COLS), BF16),
                   jax.ShapeDtypeStruct((rows, b_cols), BF16),
                   jax.ShapeDtypeStruct(wo.shape, BF16)],
        compiler_params=pltpu.CompilerParams(vmem_limit_bytes=VMEM_LIMIT),
        name="narrow",
    )(wi.T, wo)


def _mixer(x, n1w, wa, wb, wgu, bgu, gnw, lnw, lnb, ws, bst, wout, ffn_ws):
    bsz, seq, _ = x.shape
    assert seq % MIX_TILE == 0 and MIX_TILE % MIX_SUB == 0 and MIX_SUB % PAIR == 0
    n_t = seq // MIX_TILE
    consts = (n1w, wa, wb, wgu, bgu, gnw, lnw, lnb, ws, bst, wout)

    def slab_spec(w):
        rows, width = w.shape
        slab = next(r for r in range(BF16_SUBLANES, rows + 1, BF16_SUBLANES)
                    if rows % r == 0 and r * bsz * n_t >= rows)
        last = rows // slab - 1
        return pl.BlockSpec((slab, width), lambda b, t: (jnp.minimum(b * n_t + t, last), 0))

    slab_specs = [slab_spec(w) for w in ffn_ws]
    tile_spec = pl.BlockSpec((1, MIX_TILE, D_MODEL), lambda b, t: (b, t, 0))
    h, *narrowed = pl.pallas_call(
        _mixer_kernel,
        grid=(bsz, n_t),
        in_specs=[tile_spec] + [_const_spec(a.shape) for a in consts] + slab_specs,
        out_specs=[tile_spec] + slab_specs,
        out_shape=[jax.ShapeDtypeStruct(x.shape, F32)]
        + [jax.ShapeDtypeStruct(w.shape, BF16) for w in ffn_ws],
        scratch_shapes=[
            pltpu.VMEM((GLA_KEY, GLA_DV), F32),
            pltpu.VMEM((MIX_TILE, GLA_WIDTH), F32),
            pltpu.VMEM((MIX_TILE, D_MODEL), BF16),
            pltpu.VMEM((D_MODEL, GLA_KEY), BF16),
        ],
        compiler_params=pltpu.CompilerParams(
            dimension_semantics=("arbitrary", "arbitrary"),
            vmem_limit_bytes=VMEM_LIMIT),
        name="mixer",
    )(x, *consts, *ffn_ws)
    return h, narrowed


def _ffn(h, n2w, wg, wu, wd, fnw):
    m = h.shape[0]
    assert m % FFN_TILE == 0
    consts = (n2w, wg, wu, wd, fnw)
    return pl.pallas_call(
        _ffn_kernel,
        grid=(m // FFN_TILE,),
        in_specs=[pl.BlockSpec((FFN_TILE, D_MODEL), lambda i: (i, 0))]
        + [_const_spec(a.shape) for a in consts],
        out_specs=pl.BlockSpec((FFN_TILE, D_MODEL), lambda i: (i, 0)),
        out_shape=jax.ShapeDtypeStruct(h.shape, F32),
        scratch_shapes=[pltpu.VMEM((FFN_TILE, D_FF), BF16)],
        compiler_params=pltpu.CompilerParams(
            dimension_semantics=("parallel",),
            vmem_limit_bytes=VMEM_LIMIT),
        name="ffn",
    )(h, *consts)


def kernel(x, norm1_w, w_in, w_gate_up, b_gate_up, gla_norm_w, sg_ln_w, sg_ln_b, sg_w_s, sg_b_s,
           w_out, norm2_w, w_ffn_gate, w_ffn_up, w_ffn_down, final_norm_w):
    bsz, seq, _ = x.shape
    assert w_in.shape[0] == 1, "the final norm is fused into the (single) layer's ffn call"
    wa, wb, wout = _narrow_mixer_weights(w_in[0], w_out[0])
    wgu = jnp.concatenate(
        [w_gate_up[0], jnp.zeros((GLR_PAD - GATE_RANK, GLA_KEY), w_gate_up.dtype)],
        axis=0).astype(BF16)
    h, (wg, wu, wd) = _mixer(
        x, norm1_w[0][None, :], wa, wb, wgu, b_gate_up[0][None, :], gla_norm_w[0][None, :],
        sg_ln_w[0], sg_ln_b[0], sg_w_s[0].astype(BF16), jnp.transpose(sg_b_s[0]),
        wout, (w_ffn_gate[0], w_ffn_up[0], w_ffn_down[0]))
    out = _ffn(h.reshape(bsz * seq, D_MODEL), norm2_w[0][None, :], wg, wu, wd,
               final_norm_w[None, :])
    return out.reshape(bsz, seq, D_MODEL)
```

```python
import jax
import jax.numpy as jnp
from jax import lax
from jax.experimental import pallas as pl
from jax.experimental.pallas import tpu as pltpu

F32 = jnp.float32
BF16 = jnp.bfloat16

D_MODEL = 1024
GLA_HEADS = 4
GLA_DK = 64
GLA_DV = 128
GLA_KEY = GLA_HEADS * GLA_DK
GLA_WIDTH = GLA_HEADS * GLA_DV
GATE_RANK = 16
GATE_NORMALIZER = 16.0
GLA_CHUNK = 64
SG_GROUPS = 4
SG_CH = 128
SG_WIDTH = SG_GROUPS * SG_CH
SG_CHUNK = 128
D_FF = 2816
NORM_EPS = 1e-5

LANES = 128
BF16_SUBLANES = 16
GLR_PAD = LANES

C_Q = 0
C_K = C_Q + GLA_KEY
C_V = C_K + GLA_KEY
C_GLR = C_V + GLA_WIDTH
A_COLS = C_GLR + GLR_PAD
C_GOUT = A_COLS
C_SU = C_GOUT + GLA_WIDTH
C_SV = C_SU + SG_WIDTH

PAIR = 2 * GLA_CHUNK
HEADS_PER_GROUP = LANES // GLA_DK
PROJ_COLS = 256

MIX_TILE = 2048
MIX_SUB = 512
FFN_TILE = 1024
FFN_SUB = 512
FF_CHUNK = 256
DOWN_SPLIT = 2
FFN_LEAD = 1
VMEM_LIMIT = 56 * 1024 * 1024


def _rms(x, w):
    ms = jnp.mean(x * x, axis=-1, keepdims=True)
    return x * lax.rsqrt(ms + NORM_EPS) * w


def _gelu(x):
    return 0.5 * x * (1.0 + lax.erf(x * (0.5 ** 0.5)))


def _dot(a, b):
    return jnp.dot(a, b, preferred_element_type=F32)


def _mixer_kernel(x_ref, n1w_ref, wa_ref, wb_ref, wgu_ref, bgu_ref, gnw_ref, lnw_ref, lnb_ref,
                  ws_ref, bst_ref, wout_ref, f0_ref, f1_ref, f2_ref,
                  o_ref, f0b_ref, f1b_ref, f2b_ref, st_ref, ogla_ref, mix_ref, wfold_ref):
    @pl.when(pl.program_id(1) == 0)
    def _():
        st_ref[...] = jnp.zeros_like(st_ref)

    @pl.when((pl.program_id(0) == 0) & (pl.program_id(1) == 0))
    def _():
        wfold_ref[...] = _dot(wa_ref[:, C_GLR:C_GLR + GLR_PAD], wgu_ref[...]).astype(BF16)

    st = st_ref[...]
    for s in range(x_ref.shape[1] // MIX_SUB):
        st = _mixer_sub_tile(s * MIX_SUB, st, x_ref, n1w_ref, wa_ref, wb_ref, wfold_ref, bgu_ref,
                             gnw_ref, lnw_ref, lnb_ref, ws_ref, bst_ref, wout_ref, o_ref, ogla_ref,
                             mix_ref)
    st_ref[...] = st

    for src, dst in ((f0_ref, f0b_ref), (f1_ref, f1b_ref), (f2_ref, f2b_ref)):
        dst[...] = src[...].astype(BF16)


def _mixer_sub_tile(base, st, x_ref, n1w_ref, wa_ref, wb_ref, wfold_ref, bgu_ref, gnw_ref, lnw_ref,
                    lnb_ref, ws_ref, bst_ref, wout_ref, o_ref, ogla_ref, mix_ref):
    tt = MIX_SUB
    n_pairs = tt // PAIR
    per = PROJ_COLS // LANES
    rows = slice(base, base + tt)

    x = x_ref[0, rows, :]
    n = _rms(x, n1w_ref[...]).astype(BF16)

    def proj(lo, width):
        if lo < A_COLS:
            return _dot(n, wa_ref[:, lo:lo + width])
        return _dot(n, wb_ref[:, lo - A_COLS:lo - A_COLS + width])

    def proj_wide(lo, j):
        return proj(lo + j * PROJ_COLS, PROJ_COLS)

    def cols(pieces, g):
        return pieces[g // per][:, (g % per) * LANES:(g % per + 1) * LANES]

    gk = _dot(n, wfold_ref[...]) + bgu_ref[...]
    sv = [proj_wide(C_SV, j) for j in range(SG_WIDTH // PROJ_COLS)]
    log_a = (jnp.minimum(gk, 0.0) - jnp.log1p(jnp.exp(-jnp.abs(gk)))) * (1.0 / GATE_NORMALIZER)
    k = proj(C_K, GLA_KEY)
    q = proj(C_Q, GLA_KEY)

    la_hi = log_a.astype(BF16)
    la_lo = (log_a - la_hi.astype(F32)).astype(BF16)
    r = lax.broadcasted_iota(jnp.int32, (PAIR, 2 * PAIR), 0)
    c = lax.broadcasted_iota(jnp.int32, (PAIR, 2 * PAIR), 1) % PAIR
    tri2 = (r >= c).astype(BF16)
    b_parts = []
    for p in range(n_pairs):
        psl = slice(p * PAIR, (p + 1) * PAIR)
        b_parts.append(_dot(tri2, jnp.concatenate([la_hi[psl], la_lo[psl]], axis=0)))
    v = proj(C_V, GLA_WIDTH)

    y = []
    for g in range(SG_GROUPS):
        v_g = _gelu(cols(sv, g))
        mu = jnp.mean(v_g, axis=-1, keepdims=True)
        d = v_g - mu
        var = jnp.mean(d * d, axis=-1, keepdims=True)
        y.append((d * lax.rsqrt(var + NORM_EPS) * lnw_ref[g:g + 1, :]
                  + lnb_ref[g:g + 1, :]).astype(BF16))

    lane_head = lax.broadcasted_iota(jnp.int32, (PAIR, GLA_KEY), 1) // GLA_DK
    pr = lax.broadcasted_iota(jnp.int32, (GLA_HEADS * PAIR, PAIR), 0) % PAIR
    pc = lax.broadcasted_iota(jnp.int32, (GLA_HEADS * PAIR, PAIR), 1)
    causal = pr >= pc

    def gla_scores(p):
        psl = slice(p * PAIR, (p + 1) * PAIR)
        b = b_parts[p]
        b_mid = b[GLA_CHUNK - 1:GLA_CHUNK]
        b_end = b[PAIR - 1:PAIR]
        qs = q[psl] * (GLA_DK ** -0.5)
        kp = k[psl]
        q_rel = qs * jnp.exp(b - b_mid)
        k_rel_t = (kp * jnp.exp(b_mid - b)).T.astype(BF16)
        q_in = qs * jnp.exp(b)
        k_out_t = (kp * jnp.exp(b_end - b)).T.astype(BF16)
        decay_t = jnp.exp(jnp.broadcast_to(b_end, (LANES, GLA_KEY)).T)
        scores = []
        for g in range(GLA_HEADS // HEADS_PER_GROUP):
            lanes = slice(g * LANES, (g + 1) * LANES)
            q_stack = jnp.concatenate(
                [jnp.where(lane_head == h, q_rel, 0.0)[:, lanes]
                 for h in range(g * HEADS_PER_GROUP, (g + 1) * HEADS_PER_GROUP)], axis=0)
            scores.append(_dot(q_stack.astype(BF16), k_rel_t[lanes]))
        probs = jnp.where(causal, jnp.concatenate(scores, axis=0), 0.0).astype(BF16)
        q_heads = [jnp.where(lane_head == h, q_in, 0.0)[:, (h // HEADS_PER_GROUP) * LANES:
                                                         (h // HEADS_PER_GROUP + 1) * LANES].astype(BF16)
                   for h in range(GLA_HEADS)]
        return probs, q_heads, k_out_t, decay_t

    def gla_apply(p, st, probs, q_heads, k_out_t, decay_t):
        vp_b = v[p * PAIR:(p + 1) * PAIR].astype(BF16)
        st_b = st.astype(BF16)
        zeros = jnp.zeros((GLA_DK, LANES), BF16)
        o, u = [], []
        for h in range(GLA_HEADS):
            g = h // HEADS_PER_GROUP
            v_h = vp_b[:, h * GLA_DV:(h + 1) * GLA_DV]
            lhs = jnp.concatenate(
                [jnp.concatenate([probs[h * PAIR:(h + 1) * PAIR], q_heads[h]], axis=1),
                 jnp.concatenate([k_out_t[h * GLA_DK:(h + 1) * GLA_DK], zeros], axis=1)], axis=0)
            rhs = jnp.concatenate([v_h, st_b[g * LANES:(g + 1) * LANES]], axis=0)
            ou = _dot(lhs, rhs)
            o.append(ou[:PAIR])
            u.append(ou[PAIR:])
        ogla_ref[base + p * PAIR:base + (p + 1) * PAIR, :] = jnp.concatenate(o, axis=1)
        return st * decay_t + jnp.concatenate(u, axis=0)

    wr = lax.broadcasted_iota(jnp.int32, (SG_CHUNK, SG_CHUNK), 0)
    wc = lax.broadcasted_iota(jnp.int32, (SG_CHUNK, SG_CHUNK), 1)

    def sg_mix(su):
        for g in range(SG_GROUPS):
            u_g = _gelu(cols(su, g))
            w_g = jnp.where(wr >= wc, ws_ref[g], jnp.zeros((), BF16))
            bias = bst_ref[:, g:g + 1]
            for p in range(tt // SG_CHUNK):
                rs = slice(p * SG_CHUNK, (p + 1) * SG_CHUNK)
                mixed = _dot(w_g, y[g][rs]) + bias
                mix_ref[base + p * SG_CHUNK:base + (p + 1) * SG_CHUNK,
                        GLA_WIDTH + g * SG_CH:GLA_WIDTH + (g + 1) * SG_CH] = (
                            u_g[rs] * mixed).astype(BF16)

    su = [proj_wide(C_SU, j) for j in range(SG_WIDTH // PROJ_COLS)]
    go = []
    acc = []
    fills = [lambda: sg_mix(su),
             lambda: go.append(proj_wide(C_GOUT, 0)),
             lambda: go.append(proj_wide(C_GOUT, 1)),
             lambda: acc.append(x + _dot(mix_ref[rows, GLA_WIDTH:], wout_ref[GLA_WIDTH:, :]))]
    assert len(fills) == n_pairs
    staged = gla_scores(0)
    for p in range(n_pairs):
        fills[p]()
        nxt = gla_scores(p + 1) if p + 1 < n_pairs else None
        st = gla_apply(p, st, *staged)
        staged = nxt

    acc = acc[0]
    for h in range(GLA_HEADS):
        hs = slice(h * GLA_DV, (h + 1) * GLA_DV)
        o_h = _rms(ogla_ref[rows, hs], gnw_ref[...])
        g_h = cols(go, h)
        mix_ref[rows, hs] = (o_h * (g_h * jax.nn.sigmoid(g_h))).astype(BF16)
    o_ref[0, rows, :] = acc + _dot(mix_ref[rows, :GLA_WIDTH], wout_ref[:GLA_WIDTH, :])
    return st


def _ffn_kernel(h_ref, n2w_ref, wg_ref, wu_ref, wd_ref, fnw_ref, o_ref, hid_ref):
    n_sub = h_ref.shape[0] // FFN_SUB
    n_chunks = D_FF // FF_CHUNK
    subs = [slice(s * FFN_SUB, (s + 1) * FFN_SUB) for s in range(n_sub)]
    xw = [None] * n_sub
    inv = [None] * n_sub

    def gate_up(s, f):
        if xw[s] is None:
            h = h_ref[subs[s], :]
            xw[s] = (h * n2w_ref[...]).astype(BF16)
            inv[s] = lax.rsqrt(jnp.mean(h * h, axis=-1, keepdims=True) + NORM_EPS)
        fs = slice(f * FF_CHUNK, (f + 1) * FF_CHUNK)
        a = _dot(xw[s], wg_ref[:, fs]) * inv[s]
        u = _dot(xw[s], wu_ref[:, fs]) * inv[s]
        hid_ref[subs[s], fs] = (a * jax.nn.sigmoid(a) * u).astype(BF16)

    def down(s):
        piece = FFN_SUB // DOWN_SPLIT
        for j in range(DOWN_SPLIT):
            rows = slice(subs[s].start + j * piece, subs[s].start + (j + 1) * piece)
            h2 = h_ref[rows, :] + _dot(hid_ref[rows, :], wd_ref[...])
            o_ref[rows, :] = _rms(h2, fnw_ref[...])

    for s in range(n_sub):
        for f in range(FFN_LEAD if s else 0, n_chunks):
            gate_up(s, f)
        if s + 1 < n_sub:
            for f in range(FFN_LEAD):
                gate_up(s + 1, f)
        down(s)


def _const_spec(shape):
    nd = len(shape)
    return pl.BlockSpec(shape, lambda *_: (0,) * nd, pipeline_mode=pl.Buffered(1))


def _narrow_kernel(wit_ref, wo_ref, wgu_ref, ws_ref, wa_ref, wb_ref, wob_ref, wgub_ref, wsb_ref):
    def put(dst_ref, j, block):
        dst_ref[:, j * LANES:(j + 1) * LANES] = block.T.astype(BF16)

    for j in range(C_GLR // LANES):
        put(wa_ref, j, wit_ref[j * LANES:(j + 1) * LANES, :])
    row = lax.broadcasted_iota(jnp.int32, (GLR_PAD, wit_ref.shape[1]), 0)
    put(wa_ref, C_GLR // LANES,
        jnp.where(row < GATE_RANK, wit_ref[C_GLR:C_GLR + GLR_PAD, :], 0.0))
    b0 = C_GLR + GATE_RANK
    for j in range(wb_ref.shape[1] // LANES):
        put(wb_ref, j, wit_ref[b0 + j * LANES:b0 + (j + 1) * LANES, :])
    wob_ref[...] = wo_ref[...].astype(BF16)
    wgub_ref[...] = jnp.zeros_like(wgub_ref)
    wgub_ref[:GATE_RANK, :] = wgu_ref[...].astype(BF16)
    wsb_ref[...] = ws_ref[...].astype(BF16)


def _narrow_mixer_weights(wi, wo, wgu, ws):
    rows, in_cols = wi.shape
    b_cols = in_cols - C_GLR - GATE_RANK
    assert b_cols % LANES == 0 and C_GLR % LANES == 0
    return pl.pallas_call(
        _narrow_kernel,
        out_shape=[jax.ShapeDtypeStruct((rows, A_COLS), BF16),
                   jax.ShapeDtypeStruct((rows, b_cols), BF16),
                   jax.ShapeDtypeStruct(wo.shape, BF16),
                   jax.ShapeDtypeStruct((GLR_PAD, wgu.shape[1]), BF16),
                   jax.ShapeDtypeStruct(ws.shape, BF16)],
        compiler_params=pltpu.CompilerParams(vmem_limit_bytes=VMEM_LIMIT),
        name="narrow",
    )(wi.T, wo, wgu, ws)


def _mixer(x, n1w, wa, wb, wgu, bgu, gnw, lnw, lnb, ws, bst, wout, ffn_ws):
    bsz, seq, _ = x.shape
    assert seq % MIX_TILE == 0 and MIX_TILE % MIX_SUB == 0 and MIX_SUB % PAIR == 0
    n_t = seq // MIX_TILE
    consts = (n1w, wa, wb, wgu, bgu, gnw, lnw, lnb, ws, bst, wout)

    def slab_spec(w):
        rows, width = w.shape
        slab = next(r for r in range(BF16_SUBLANES, rows + 1, BF16_SUBLANES)
                    if rows % r == 0 and r * bsz * n_t >= rows)
        last = rows // slab - 1
        return pl.BlockSpec((slab, width), lambda b, t: (jnp.minimum(b * n_t + t, last), 0))

    slab_specs = [slab_spec(w) for w in ffn_ws]
    tile_spec = pl.BlockSpec((1, MIX_TILE, D_MODEL), lambda b, t: (b, t, 0))
    h, *narrowed = pl.pallas_call(
        _mixer_kernel,
        grid=(bsz, n_t),
        in_specs=[tile_spec] + [_const_spec(a.shape) for a in consts] + slab_specs,
        out_specs=[tile_spec] + slab_specs,
        out_shape=[jax.ShapeDtypeStruct(x.shape, F32)]
        + [jax.ShapeDtypeStruct(w.shape, BF16) for w in ffn_ws],
        scratch_shapes=[
            pltpu.VMEM((GLA_KEY, GLA_DV), F32),
            pltpu.VMEM((MIX_TILE, GLA_WIDTH), F32),
            pltpu.VMEM((MIX_TILE, D_MODEL), BF16),
            pltpu.VMEM((D_MODEL, GLA_KEY), BF16),
        ],
        compiler_params=pltpu.CompilerParams(
            dimension_semantics=("arbitrary", "arbitrary"),
            vmem_limit_bytes=VMEM_LIMIT),
        name="mixer",
    )(x, *consts, *ffn_ws)
    return h, narrowed


def _ffn(h, n2w, wg, wu, wd, fnw):
    m = h.shape[0]
    assert m % FFN_TILE == 0
    consts = (n2w, wg, wu, wd, fnw)
    return pl.pallas_call(
        _ffn_kernel,
        grid=(m // FFN_TILE,),
        in_specs=[pl.BlockSpec((FFN_TILE, D_MODEL), lambda i: (i, 0))]
        + [_const_spec(a.shape) for a in consts],
        out_specs=pl.BlockSpec((FFN_TILE, D_MODEL), lambda i: (i, 0)),
        out_shape=jax.ShapeDtypeStruct(h.shape, F32),
        scratch_shapes=[pltpu.VMEM((FFN_TILE, D_FF), BF16)],
        compiler_params=pltpu.CompilerParams(
            dimension_semantics=("parallel",),
            vmem_limit_bytes=VMEM_LIMIT),
        name="ffn",
    )(h, *consts)


def kernel(x, norm1_w, w_in, w_gate_up, b_gate_up, gla_norm_w, sg_ln_w, sg_ln_b, sg_w_s, sg_b_s,
           w_out, norm2_w, w_ffn_gate, w_ffn_up, w_ffn_down, final_norm_w):
    bsz, seq, _ = x.shape
    assert w_in.shape[0] == 1, "the final norm is fused into the (single) layer's ffn call"
    wa, wb, wout, wgu, ws = _narrow_mixer_weights(w_in[0], w_out[0], w_gate_up[0], sg_w_s[0])
    h, (wg, wu, wd) = _mixer(
        x, norm1_w[0][None, :], wa, wb, wgu, b_gate_up[0][None, :], gla_norm_w[0][None, :],
        sg_ln_w[0], sg_ln_b[0], ws, jnp.transpose(sg_b_s[0]),
        wout, (w_ffn_gate[0], w_ffn_up[0], w_ffn_down[0]))
    out = _ffn(h.reshape(bsz * seq, D_MODEL), norm2_w[0][None, :], wg, wu, wd,
               final_norm_w[None, :])
    return out.reshape(bsz, seq, D_MODEL)
```

```python
import jax
import jax.numpy as jnp
from jax import lax
from jax.experimental import pallas as pl
from jax.experimental.pallas import tpu as pltpu

F32 = jnp.float32
BF16 = jnp.bfloat16

D_MODEL = 1024
GLA_HEADS = 4
GLA_DK = 64
GLA_DV = 128
GLA_KEY = GLA_HEADS * GLA_DK
GLA_WIDTH = GLA_HEADS * GLA_DV
GATE_RANK = 16
GATE_NORMALIZER = 16.0
GLA_CHUNK = 64
SG_GROUPS = 4
SG_CH = 128
SG_WIDTH = SG_GROUPS * SG_CH
SG_CHUNK = 128
D_FF = 2816
NORM_EPS = 1e-5

LANES = 128
BF16_SUBLANES = 16
GLR_PAD = LANES

C_Q = 0
C_K = C_Q + GLA_KEY
C_V = C_K + GLA_KEY
C_GLR = C_V + GLA_WIDTH
A_COLS = C_GLR + GLR_PAD
C_GOUT = A_COLS
C_SU = C_GOUT + GLA_WIDTH
C_SV = C_SU + SG_WIDTH

PAIR = 2 * GLA_CHUNK
HEADS_PER_GROUP = LANES // GLA_DK
PROJ_COLS = 256

MIX_TILE = 1024
MIX_SUB = 512
FFN_TILE = 1024
FFN_SUB = 512
FF_CHUNK = 256
DOWN_SPLIT = 2
FFN_LEAD = 1
VMEM_LIMIT = 56 * 1024 * 1024


def _rms(x, w):
    ms = jnp.mean(x * x, axis=-1, keepdims=True)
    return x * lax.rsqrt(ms + NORM_EPS) * w


def _gelu(x):
    return 0.5 * x * (1.0 + lax.erf(x * (0.5 ** 0.5)))


def _dot(a, b):
    return jnp.dot(a, b, preferred_element_type=F32)


def _mixer_kernel(x_ref, n1w_ref, wa_ref, wb_ref, wgu_ref, bgu_ref, gnw_ref, lnw_ref, lnb_ref,
                  ws_ref, bst_ref, wout_ref, f0_ref, f1_ref, f2_ref,
                  o_ref, f0b_ref, f1b_ref, f2b_ref, st_ref, ogla_ref, mix_ref, wfold_ref):
    @pl.when(pl.program_id(1) == 0)
    def _():
        st_ref[...] = jnp.zeros_like(st_ref)

    @pl.when((pl.program_id(0) == 0) & (pl.program_id(1) == 0))
    def _():
        wfold_ref[...] = _dot(wa_ref[:, C_GLR:C_GLR + GLR_PAD], wgu_ref[...]).astype(BF16)

    st = st_ref[...]
    for s in range(x_ref.shape[1] // MIX_SUB):
        st = _mixer_sub_tile(s * MIX_SUB, st, x_ref, n1w_ref, wa_ref, wb_ref, wfold_ref, bgu_ref,
                             gnw_ref, lnw_ref, lnb_ref, ws_ref, bst_ref, wout_ref, o_ref, ogla_ref,
                             mix_ref)
    st_ref[...] = st

    for src, dst in ((f0_ref, f0b_ref), (f1_ref, f1b_ref), (f2_ref, f2b_ref)):
        dst[...] = src[...].astype(BF16)


def _mixer_sub_tile(base, st, x_ref, n1w_ref, wa_ref, wb_ref, wfold_ref, bgu_ref, gnw_ref, lnw_ref,
                    lnb_ref, ws_ref, bst_ref, wout_ref, o_ref, ogla_ref, mix_ref):
    tt = MIX_SUB
    n_pairs = tt // PAIR
    per = PROJ_COLS // LANES
    rows = slice(base, base + tt)

    x = x_ref[0, rows, :]
    n = _rms(x, n1w_ref[...]).astype(BF16)

    def proj(lo, width):
        if lo < A_COLS:
            return _dot(n, wa_ref[:, lo:lo + width])
        return _dot(n, wb_ref[:, lo - A_COLS:lo - A_COLS + width])

    def proj_wide(lo, j):
        return proj(lo + j * PROJ_COLS, PROJ_COLS)

    def cols(pieces, g):
        return pieces[g // per][:, (g % per) * LANES:(g % per + 1) * LANES]

    gk = _dot(n, wfold_ref[...]) + bgu_ref[...]
    sv = [proj_wide(C_SV, j) for j in range(SG_WIDTH // PROJ_COLS)]
    log_a = (jnp.minimum(gk, 0.0) - jnp.log1p(jnp.exp(-jnp.abs(gk)))) * (1.0 / GATE_NORMALIZER)
    k = proj(C_K, GLA_KEY)
    q = proj(C_Q, GLA_KEY)

    la_hi = log_a.astype(BF16)
    la_lo = (log_a - la_hi.astype(F32)).astype(BF16)
    r = lax.broadcasted_iota(jnp.int32, (PAIR, 2 * PAIR), 0)
    c = lax.broadcasted_iota(jnp.int32, (PAIR, 2 * PAIR), 1) % PAIR
    tri2 = (r >= c).astype(BF16)
    b_parts = []
    for p in range(n_pairs):
        psl = slice(p * PAIR, (p + 1) * PAIR)
        b_parts.append(_dot(tri2, jnp.concatenate([la_hi[psl], la_lo[psl]], axis=0)))
    v = proj(C_V, GLA_WIDTH)

    y = []
    for g in range(SG_GROUPS):
        v_g = _gelu(cols(sv, g))
        mu = jnp.mean(v_g, axis=-1, keepdims=True)
        d = v_g - mu
        var = jnp.mean(d * d, axis=-1, keepdims=True)
        y.append((d * lax.rsqrt(var + NORM_EPS) * lnw_ref[g:g + 1, :]
                  + lnb_ref[g:g + 1, :]).astype(BF16))

    lane_head = lax.broadcasted_iota(jnp.int32, (PAIR, GLA_KEY), 1) // GLA_DK
    pr = lax.broadcasted_iota(jnp.int32, (GLA_HEADS * PAIR, PAIR), 0) % PAIR
    pc = lax.broadcasted_iota(jnp.int32, (GLA_HEADS * PAIR, PAIR), 1)
    causal = pr >= pc

    def gla_scores(p):
        psl = slice(p * PAIR, (p + 1) * PAIR)
        b = b_parts[p]
        b_mid = b[GLA_CHUNK - 1:GLA_CHUNK]
        b_end = b[PAIR - 1:PAIR]
        qs = q[psl] * (GLA_DK ** -0.5)
        kp = k[psl]
        q_rel = qs * jnp.exp(b - b_mid)
        k_rel_t = (kp * jnp.exp(b_mid - b)).T.astype(BF16)
        q_in = qs * jnp.exp(b)
        k_out_t = (kp * jnp.exp(b_end - b)).T.astype(BF16)
        decay_t = jnp.exp(jnp.broadcast_to(b_end, (LANES, GLA_KEY)).T)
        scores = []
        for g in range(GLA_HEADS // HEADS_PER_GROUP):
            lanes = slice(g * LANES, (g + 1) * LANES)
            q_stack = jnp.concatenate(
                [jnp.where(lane_head == h, q_rel, 0.0)[:, lanes]
                 for h in range(g * HEADS_PER_GROUP, (g + 1) * HEADS_PER_GROUP)], axis=0)
            scores.append(_dot(q_stack.astype(BF16), k_rel_t[lanes]))
        probs = jnp.where(causal, jnp.concatenate(scores, axis=0), 0.0).astype(BF16)
        q_heads = [jnp.where(lane_head == h, q_in, 0.0)[:, (h // HEADS_PER_GROUP) * LANES:
                                                         (h // HEADS_PER_GROUP + 1) * LANES].astype(BF16)
                   for h in range(GLA_HEADS)]
        return probs, q_heads, k_out_t, decay_t

    def gla_apply(p, st, probs, q_heads, k_out_t, decay_t):
        vp_b = v[p * PAIR:(p + 1) * PAIR].astype(BF16)
        st_b = st.astype(BF16)
        zeros = jnp.zeros((GLA_DK, LANES), BF16)
        o, u = [], []
        for h in range(GLA_HEADS):
            g = h // HEADS_PER_GROUP
            v_h = vp_b[:, h * GLA_DV:(h + 1) * GLA_DV]
            lhs = jnp.concatenate(
                [jnp.concatenate([probs[h * PAIR:(h + 1) * PAIR], q_heads[h]], axis=1),
                 jnp.concatenate([k_out_t[h * GLA_DK:(h + 1) * GLA_DK], zeros], axis=1)], axis=0)
            rhs = jnp.concatenate([v_h, st_b[g * LANES:(g + 1) * LANES]], axis=0)
            ou = _dot(lhs, rhs)
            o.append(ou[:PAIR])
            u.append(ou[PAIR:])
        ogla_ref[base + p * PAIR:base + (p + 1) * PAIR, :] = jnp.concatenate(o, axis=1)
        return st * decay_t + jnp.concatenate(u, axis=0)

    wr = lax.broadcasted_iota(jnp.int32, (SG_CHUNK, SG_CHUNK), 0)
    wc = lax.broadcasted_iota(jnp.int32, (SG_CHUNK, SG_CHUNK), 1)

    def sg_mix(su):
        for g in range(SG_GROUPS):
            u_g = _gelu(cols(su, g))
            w_g = jnp.where(wr >= wc, ws_ref[g], jnp.zeros((), BF16))
            bias = bst_ref[:, g:g + 1]
            for p in range(tt // SG_CHUNK):
                rs = slice(p * SG_CHUNK, (p + 1) * SG_CHUNK)
                mixed = _dot(w_g, y[g][rs]) + bias
                mix_ref[base + p * SG_CHUNK:base + (p + 1) * SG_CHUNK,
                        GLA_WIDTH + g * SG_CH:GLA_WIDTH + (g + 1) * SG_CH] = (
                            u_g[rs] * mixed).astype(BF16)

    su = [proj_wide(C_SU, j) for j in range(SG_WIDTH // PROJ_COLS)]
    go = []
    acc = []
    fills = [lambda: sg_mix(su),
             lambda: go.append(proj_wide(C_GOUT, 0)),
             lambda: go.append(proj_wide(C_GOUT, 1)),
             lambda: acc.append(x + _dot(mix_ref[rows, GLA_WIDTH:], wout_ref[GLA_WIDTH:, :]))]
    assert len(fills) == n_pairs
    staged = gla_scores(0)
    for p in range(n_pairs):
        fills[p]()
        nxt = gla_scores(p + 1) if p + 1 < n_pairs else None
        st = gla_apply(p, st, *staged)
        staged = nxt

    acc = acc[0]
    for h in range(GLA_HEADS):
        hs = slice(h * GLA_DV, (h + 1) * GLA_DV)
        o_h = _rms(ogla_ref[rows, hs], gnw_ref[...])
        g_h = cols(go, h)
        mix_ref[rows, hs] = (o_h * (g_h * jax.nn.sigmoid(g_h))).astype(BF16)
    o_ref[0, rows, :] = acc + _dot(mix_ref[rows, :GLA_WIDTH], wout_ref[:GLA_WIDTH, :])
    return st


def _ffn_kernel(h_ref, n2w_ref, wg_ref, wu_ref, wd_ref, fnw_ref, o_ref, hid_ref):
    n_sub = h_ref.shape[0] // FFN_SUB
    n_chunks = D_FF // FF_CHUNK
    subs = [slice(s * FFN_SUB, (s + 1) * FFN_SUB) for s in range(n_sub)]
    xw = [None] * n_sub
    inv = [None] * n_sub

    def gate_up(s, f):
        if xw[s] is None:
            h = h_ref[subs[s], :]
            xw[s] = (h * n2w_ref[...]).astype(BF16)
            inv[s] = lax.rsqrt(jnp.mean(h * h, axis=-1, keepdims=True) + NORM_EPS)
        fs = slice(f * FF_CHUNK, (f + 1) * FF_CHUNK)
        a = _dot(xw[s], wg_ref[:, fs]) * inv[s]
        u = _dot(xw[s], wu_ref[:, fs]) * inv[s]
        hid_ref[subs[s], fs] = (a * jax.nn.sigmoid(a) * u).astype(BF16)

    def down(s):
        piece = FFN_SUB // DOWN_SPLIT
        for j in range(DOWN_SPLIT):
            rows = slice(subs[s].start + j * piece, subs[s].start + (j + 1) * piece)
            h2 = h_ref[rows, :] + _dot(hid_ref[rows, :], wd_ref[...])
            o_ref[rows, :] = _rms(h2, fnw_ref[...])

    for s in range(n_sub):
        for f in range(FFN_LEAD if s else 0, n_chunks):
            gate_up(s, f)
        if s + 1 < n_sub:
            for f in range(FFN_LEAD):
                gate_up(s + 1, f)
        down(s)


def _const_spec(shape):
    nd = len(shape)
    return pl.BlockSpec(shape, lambda *_: (0,) * nd, pipeline_mode=pl.Buffered(1))


def _narrow_kernel(wit_ref, wo_ref, wgu_ref, ws_ref, wa_ref, wb_ref, wob_ref, wgub_ref, wsb_ref):
    def put(dst_ref, j, block):
        dst_ref[:, j * LANES:(j + 1) * LANES] = block.T.astype(BF16)

    for j in range(C_GLR // LANES):
        put(wa_ref, j, wit_ref[j * LANES:(j + 1) * LANES, :])
    row = lax.broadcasted_iota(jnp.int32, (GLR_PAD, wit_ref.shape[1]), 0)
    put(wa_ref, C_GLR // LANES,
        jnp.where(row < GATE_RANK, wit_ref[C_GLR:C_GLR + GLR_PAD, :], 0.0))
    b0 = C_GLR + GATE_RANK
    for j in range(wb_ref.shape[1] // LANES):
        put(wb_ref, j, wit_ref[b0 + j * LANES:b0 + (j + 1) * LANES, :])
    wob_ref[...] = wo_ref[...].astype(BF16)
    wgub_ref[...] = jnp.zeros_like(wgub_ref)
    wgub_ref[:GATE_RANK, :] = wgu_ref[...].astype(BF16)
    wsb_ref[...] = ws_ref[...].astype(BF16)


def _narrow_mixer_weights(wi, wo, wgu, ws):
    rows, in_cols = wi.shape
    b_cols = in_cols - C_GLR - GATE_RANK
    assert b_cols % LANES == 0 and C_GLR % LANES == 0
    return pl.pallas_call(
        _narrow_kernel,
        out_shape=[jax.ShapeDtypeStruct((rows, A_COLS), BF16),
                   jax.ShapeDtypeStruct((rows, b_cols), BF16),
                   jax.ShapeDtypeStruct(wo.shape, BF16),
                   jax.ShapeDtypeStruct((GLR_PAD, wgu.shape[1]), BF16),
                   jax.ShapeDtypeStruct(ws.shape, BF16)],
        compiler_params=pltpu.CompilerParams(vmem_limit_bytes=VMEM_LIMIT),
        name="narrow",
    )(wi.T, wo, wgu, ws)


def _mixer(x, n1w, wa, wb, wgu, bgu, gnw, lnw, lnb, ws, bst, wout, ffn_ws):
    bsz, seq, _ = x.shape
    assert seq % MIX_TILE == 0 and MIX_TILE % MIX_SUB == 0 and MIX_SUB % PAIR == 0
    n_t = seq // MIX_TILE
    consts = (n1w, wa, wb, wgu, bgu, gnw, lnw, lnb, ws, bst, wout)

    def slab_spec(w):
        rows, width = w.shape
        slab = next(r for r in range(BF16_SUBLANES, rows + 1, BF16_SUBLANES)
                    if rows % r == 0 and r * bsz * n_t >= rows)
        last = rows // slab - 1
        return pl.BlockSpec((slab, width), lambda b, t: (jnp.minimum(b * n_t + t, last), 0))

    slab_specs = [slab_spec(w) for w in ffn_ws]
    tile_spec = pl.BlockSpec((1, MIX_TILE, D_MODEL), lambda b, t: (b, t, 0))
    h, *narrowed = pl.pallas_call(
        _mixer_kernel,
        grid=(bsz, n_t),
        in_specs=[tile_spec] + [_const_spec(a.shape) for a in consts] + slab_specs,
        out_specs=[tile_spec] + slab_specs,
        out_shape=[jax.ShapeDtypeStruct(x.shape, F32)]
        + [jax.ShapeDtypeStruct(w.shape, BF16) for w in ffn_ws],
        scratch_shapes=[
            pltpu.VMEM((GLA_KEY, GLA_DV), F32),
            pltpu.VMEM((MIX_TILE, GLA_WIDTH), F32),
            pltpu.VMEM((MIX_TILE, D_MODEL), BF16),
            pltpu.VMEM((D_MODEL, GLA_KEY), BF16),
        ],
        compiler_params=pltpu.CompilerParams(
            dimension_semantics=("arbitrary", "arbitrary"),
            vmem_limit_bytes=VMEM_LIMIT),
        name="mixer",
    )(x, *consts, *ffn_ws)
    return h, narrowed


def _ffn(h, n2w, wg, wu, wd, fnw):
    m = h.shape[0]
    assert m % FFN_TILE == 0
    consts = (n2w, wg, wu, wd, fnw)
    return pl.pallas_call(
        _ffn_kernel,
        grid=(m // FFN_TILE,),
        in_specs=[pl.BlockSpec((FFN_TILE, D_MODEL), lambda i: (i, 0))]
        + [_const_spec(a.shape) for a in consts],
        out_specs=pl.BlockSpec((FFN_TILE, D_MODEL), lambda i: (i, 0)),
        out_shape=jax.ShapeDtypeStruct(h.shape, F32),
        scratch_shapes=[pltpu.VMEM((FFN_TILE, D_FF), BF16)],
        compiler_params=pltpu.CompilerParams(
            dimension_semantics=("parallel",),
            vmem_limit_bytes=VMEM_LIMIT),
        name="ffn",
    )(h, *consts)


def kernel(x, norm1_w, w_in, w_gate_up, b_gate_up, gla_norm_w, sg_ln_w, sg_ln_b, sg_w_s, sg_b_s,
           w_out, norm2_w, w_ffn_gate, w_ffn_up, w_ffn_down, final_norm_w):
    bsz, seq, _ = x.shape
    assert w_in.shape[0] == 1, "the final norm is fused into the (single) layer's ffn call"
    wa, wb, wout, wgu, ws = _narrow_mixer_weights(w_in[0], w_out[0], w_gate_up[0], sg_w_s[0])
    h, (wg, wu, wd) = _mixer(
        x, norm1_w[0][None, :], wa, wb, wgu, b_gate_up[0][None, :], gla_norm_w[0][None, :],
        sg_ln_w[0], sg_ln_b[0], ws, jnp.transpose(sg_b_s[0]),
        wout, (w_ffn_gate[0], w_ffn_up[0], w_ffn_down[0]))
    out = _ffn(h.reshape(bsz * seq, D_MODEL), norm2_w[0][None, :], wg, wu, wd,
               final_norm_w[None, :])
    return out.reshape(bsz, seq, D_MODEL)
```

```python
import jax
import jax.numpy as jnp
from jax import lax
from jax.experimental import pallas as pl
from jax.experimental.pallas import tpu as pltpu

F32 = jnp.float32
BF16 = jnp.bfloat16

D_MODEL = 1024
GLA_HEADS = 4
GLA_DK = 64
GLA_DV = 128
GLA_KEY = GLA_HEADS * GLA_DK
GLA_WIDTH = GLA_HEADS * GLA_DV
GATE_RANK = 16
GATE_NORMALIZER = 16.0
GLA_CHUNK = 64
SG_GROUPS = 4
SG_CH = 128
SG_WIDTH = SG_GROUPS * SG_CH
SG_CHUNK = 128
D_FF = 2816
NORM_EPS = 1e-5

LANES = 128
BF16_SUBLANES = 16
GLR_PAD = LANES

C_Q = 0
C_K = C_Q + GLA_KEY
C_V = C_K + GLA_KEY
C_GLR = C_V + GLA_WIDTH
A_COLS = C_GLR + GLR_PAD
C_GOUT = A_COLS
C_SU = C_GOUT + GLA_WIDTH
C_SV = C_SU + SG_WIDTH

PAIR = 2 * GLA_CHUNK
HEADS_PER_GROUP = LANES // GLA_DK
PROJ_COLS = 256

MIX_TILE = 2048
MIX_SUB = 512
FFN_TILE = 1024
FFN_SUB = 512
FF_CHUNK = 256
DOWN_SPLIT = 2
FFN_LEAD = 1
V7X_VMEM_BYTES = 64 * 1024 * 1024
VMEM_LIMIT = V7X_VMEM_BYTES - 8 * 1024 * 1024


def _rms(x, w):
    ms = jnp.mean(x * x, axis=-1, keepdims=True)
    return x * lax.rsqrt(ms + NORM_EPS) * w


def _gelu(x):
    return 0.5 * x * (1.0 + lax.erf(x * (0.5 ** 0.5)))


def _dot(a, b):
    return jnp.dot(a, b, preferred_element_type=F32)


def _mixer_kernel(x_ref, n1w_ref, wa_ref, wb_ref, wgu_ref, bgu_ref, gnw_ref, lnw_ref, lnb_ref,
                  ws_ref, bst_ref, wout_ref, f0_ref, f1_ref, f2_ref,
                  o_ref, f0b_ref, f1b_ref, f2b_ref, st_ref, ogla_ref, mix_ref, wfold_ref):
    @pl.when(pl.program_id(1) == 0)
    def _():
        st_ref[...] = jnp.zeros_like(st_ref)

    @pl.when((pl.program_id(0) == 0) & (pl.program_id(1) == 0))
    def _():
        wfold_ref[...] = _dot(wa_ref[:, C_GLR:C_GLR + GLR_PAD], wgu_ref[...]).astype(BF16)

    st = st_ref[...]
    for s in range(x_ref.shape[1] // MIX_SUB):
        st = _mixer_sub_tile(s * MIX_SUB, st, x_ref, n1w_ref, wa_ref, wb_ref, wfold_ref, bgu_ref,
                             gnw_ref, lnw_ref, lnb_ref, ws_ref, bst_ref, wout_ref, o_ref, ogla_ref,
                             mix_ref)
    st_ref[...] = st

    for src, dst in ((f0_ref, f0b_ref), (f1_ref, f1b_ref), (f2_ref, f2b_ref)):
        dst[...] = src[...].astype(BF16)


def _mixer_sub_tile(base, st, x_ref, n1w_ref, wa_ref, wb_ref, wfold_ref, bgu_ref, gnw_ref, lnw_ref,
                    lnb_ref, ws_ref, bst_ref, wout_ref, o_ref, ogla_ref, mix_ref):
    tt = MIX_SUB
    n_pairs = tt // PAIR
    per = PROJ_COLS // LANES
    rows = slice(base, base + tt)

    x = x_ref[0, rows, :]
    n = _rms(x, n1w_ref[...]).astype(BF16)

    def proj(lo, width):
        if lo < A_COLS:
            return _dot(n, wa_ref[:, lo:lo + width])
        return _dot(n, wb_ref[:, lo - A_COLS:lo - A_COLS + width])

    def proj_wide(lo, j):
        return proj(lo + j * PROJ_COLS, PROJ_COLS)

    def cols(pieces, g):
        return pieces[g // per][:, (g % per) * LANES:(g % per + 1) * LANES]

    gk = _dot(n, wfold_ref[...]) + bgu_ref[...]
    sv = [proj_wide(C_SV, j) for j in range(SG_WIDTH // PROJ_COLS)]
    log_a = (jnp.minimum(gk, 0.0) - jnp.log1p(jnp.exp(-jnp.abs(gk)))) * (1.0 / GATE_NORMALIZER)
    k = proj(C_K, GLA_KEY)
    q = proj(C_Q, GLA_KEY)

    la_hi = log_a.astype(BF16)
    la_lo = (log_a - la_hi.astype(F32)).astype(BF16)
    r = lax.broadcasted_iota(jnp.int32, (PAIR, 2 * PAIR), 0)
    c = lax.broadcasted_iota(jnp.int32, (PAIR, 2 * PAIR), 1) % PAIR
    tri2 = (r >= c).astype(BF16)
    b_parts = []
    for p in range(n_pairs):
        psl = slice(p * PAIR, (p + 1) * PAIR)
        b_parts.append(_dot(tri2, jnp.concatenate([la_hi[psl], la_lo[psl]], axis=0)))
    v = proj(C_V, GLA_WIDTH)

    y = []
    for g in range(SG_GROUPS):
        v_g = _gelu(cols(sv, g))
        mu = jnp.mean(v_g, axis=-1, keepdims=True)
        d = v_g - mu
        var = jnp.mean(d * d, axis=-1, keepdims=True)
        y.append((d * lax.rsqrt(var + NORM_EPS) * lnw_ref[g:g + 1, :]
                  + lnb_ref[g:g + 1, :]).astype(BF16))

    lane_head = lax.broadcasted_iota(jnp.int32, (PAIR, GLA_KEY), 1) // GLA_DK
    pr = lax.broadcasted_iota(jnp.int32, (GLA_HEADS * PAIR, PAIR), 0) % PAIR
    pc = lax.broadcasted_iota(jnp.int32, (GLA_HEADS * PAIR, PAIR), 1)
    causal = pr >= pc

    def gla_scores(p):
        psl = slice(p * PAIR, (p + 1) * PAIR)
        b = b_parts[p]
        b_mid = b[GLA_CHUNK - 1:GLA_CHUNK]
        b_end = b[PAIR - 1:PAIR]
        qs = q[psl] * (GLA_DK ** -0.5)
        kp = k[psl]
        q_rel = qs * jnp.exp(b - b_mid)
        k_rel_t = (kp * jnp.exp(b_mid - b)).T.astype(BF16)
        q_in = qs * jnp.exp(b)
        k_out_t = (kp * jnp.exp(b_end - b)).T.astype(BF16)
        decay_t = jnp.exp(jnp.broadcast_to(b_end, (LANES, GLA_KEY)).T)
        scores = []
        for g in range(GLA_HEADS // HEADS_PER_GROUP):
            lanes = slice(g * LANES, (g + 1) * LANES)
            q_stack = jnp.concatenate(
                [jnp.where(lane_head == h, q_rel, 0.0)[:, lanes]
                 for h in range(g * HEADS_PER_GROUP, (g + 1) * HEADS_PER_GROUP)], axis=0)
            scores.append(_dot(q_stack.astype(BF16), k_rel_t[lanes]))
        probs = jnp.where(causal, jnp.concatenate(scores, axis=0), 0.0).astype(BF16)
        q_heads = [jnp.where(lane_head == h, q_in, 0.0)[:, (h // HEADS_PER_GROUP) * LANES:
                                                         (h // HEADS_PER_GROUP + 1) * LANES].astype(BF16)
                   for h in range(GLA_HEADS)]
        return probs, q_heads, k_out_t, decay_t

    def gla_apply(p, st, probs, q_heads, k_out_t, decay_t):
        vp_b = v[p * PAIR:(p + 1) * PAIR].astype(BF16)
        st_b = st.astype(BF16)
        zeros = jnp.zeros((GLA_DK, LANES), BF16)
        o, u = [], []
        for h in range(GLA_HEADS):
            g = h // HEADS_PER_GROUP
            v_h = vp_b[:, h * GLA_DV:(h + 1) * GLA_DV]
            lhs = jnp.concatenate(
                [jnp.concatenate([probs[h * PAIR:(h + 1) * PAIR], q_heads[h]], axis=1),
                 jnp.concatenate([k_out_t[h * GLA_DK:(h + 1) * GLA_DK], zeros], axis=1)], axis=0)
            rhs = jnp.concatenate([v_h, st_b[g * LANES:(g + 1) * LANES]], axis=0)
            ou = _dot(lhs, rhs)
            o.append(ou[:PAIR])
            u.append(ou[PAIR:])
        ogla_ref[base + p * PAIR:base + (p + 1) * PAIR, :] = jnp.concatenate(o, axis=1)
        return st * decay_t + jnp.concatenate(u, axis=0)

    wr = lax.broadcasted_iota(jnp.int32, (SG_CHUNK, SG_CHUNK), 0)
    wc = lax.broadcasted_iota(jnp.int32, (SG_CHUNK, SG_CHUNK), 1)

    def sg_mix(su):
        for g in range(SG_GROUPS):
            u_g = _gelu(cols(su, g))
            w_g = jnp.where(wr >= wc, ws_ref[g], jnp.zeros((), BF16))
            bias = bst_ref[:, g:g + 1]
            for p in range(tt // SG_CHUNK):
                rs = slice(p * SG_CHUNK, (p + 1) * SG_CHUNK)
                mixed = _dot(w_g, y[g][rs]) + bias
                mix_ref[base + p * SG_CHUNK:base + (p + 1) * SG_CHUNK,
                        GLA_WIDTH + g * SG_CH:GLA_WIDTH + (g + 1) * SG_CH] = (
                            u_g[rs] * mixed).astype(BF16)

    su = [proj_wide(C_SU, j) for j in range(SG_WIDTH // PROJ_COLS)]
    go = []
    acc = []
    fills = [lambda: sg_mix(su),
             lambda: go.append(proj_wide(C_GOUT, 0)),
             lambda: go.append(proj_wide(C_GOUT, 1)),
             lambda: acc.append(x + _dot(mix_ref[rows, GLA_WIDTH:], wout_ref[GLA_WIDTH:, :]))]
    assert len(fills) == n_pairs
    staged = gla_scores(0)
    for p in range(n_pairs):
        fills[p]()
        nxt = gla_scores(p + 1) if p + 1 < n_pairs else None
        st = gla_apply(p, st, *staged)
        staged = nxt

    acc = acc[0]
    for h in range(GLA_HEADS):
        hs = slice(h * GLA_DV, (h + 1) * GLA_DV)
        o_h = _rms(ogla_ref[rows, hs], gnw_ref[...])
        g_h = cols(go, h)
        mix_ref[rows, hs] = (o_h * (g_h * jax.nn.sigmoid(g_h))).astype(BF16)
    o_ref[0, rows, :] = acc + _dot(mix_ref[rows, :GLA_WIDTH], wout_ref[:GLA_WIDTH, :])
    return st


def _ffn_kernel(h_ref, n2w_ref, wg_ref, wu_ref, wd_ref, fnw_ref, o_ref, hid_ref):
    n_sub = h_ref.shape[0] // FFN_SUB
    n_chunks = D_FF // FF_CHUNK
    subs = [slice(s * FFN_SUB, (s + 1) * FFN_SUB) for s in range(n_sub)]
    xw = [None] * n_sub
    inv = [None] * n_sub

    def gate_up(s, f):
        if xw[s] is None:
            h = h_ref[subs[s], :]
            xw[s] = (h * n2w_ref[...]).astype(BF16)
            inv[s] = lax.rsqrt(jnp.mean(h * h, axis=-1, keepdims=True) + NORM_EPS)
        fs = slice(f * FF_CHUNK, (f + 1) * FF_CHUNK)
        a = _dot(xw[s], wg_ref[:, fs]) * inv[s]
        u = _dot(xw[s], wu_ref[:, fs]) * inv[s]
        hid_ref[subs[s], fs] = (a * jax.nn.sigmoid(a) * u).astype(BF16)

    def down(s):
        piece = FFN_SUB // DOWN_SPLIT
        for j in range(DOWN_SPLIT):
            rows = slice(subs[s].start + j * piece, subs[s].start + (j + 1) * piece)
            h2 = h_ref[rows, :] + _dot(hid_ref[rows, :], wd_ref[...])
            o_ref[rows, :] = _rms(h2, fnw_ref[...])

    for s in range(n_sub):
        for f in range(FFN_LEAD if s else 0, n_chunks):
            gate_up(s, f)
        if s + 1 < n_sub:
            for f in range(FFN_LEAD):
                gate_up(s + 1, f)
        down(s)


def _const_spec(shape):
    nd = len(shape)
    return pl.BlockSpec(shape, lambda *_: (0,) * nd, pipeline_mode=pl.Buffered(1))


def _narrow_kernel(wit_ref, wo_ref, wgu_ref, ws_ref, wa_ref, wb_ref, wob_ref, wgub_ref, wsb_ref):
    def put(dst_ref, j, block):
        dst_ref[:, j * LANES:(j + 1) * LANES] = block.T.astype(BF16)

    for j in range(C_GLR // LANES):
        put(wa_ref, j, wit_ref[j * LANES:(j + 1) * LANES, :])
    row = lax.broadcasted_iota(jnp.int32, (GLR_PAD, wit_ref.shape[1]), 0)
    put(wa_ref, C_GLR // LANES,
        jnp.where(row < GATE_RANK, wit_ref[C_GLR:C_GLR + GLR_PAD, :], 0.0))
    b0 = C_GLR + GATE_RANK
    for j in range(wb_ref.shape[1] // LANES):
        put(wb_ref, j, wit_ref[b0 + j * LANES:b0 + (j + 1) * LANES, :])
    wob_ref[...] = wo_ref[...].astype(BF16)
    wgub_ref[...] = jnp.zeros_like(wgub_ref)
    wgub_ref[:GATE_RANK, :] = wgu_ref[...].astype(BF16)
    wsb_ref[...] = ws_ref[...].astype(BF16)


def _narrow_mixer_weights(wi, wo, wgu, ws):
    rows, in_cols = wi.shape
    b_cols = in_cols - C_GLR - GATE_RANK
    assert b_cols % LANES == 0 and C_GLR % LANES == 0
    return pl.pallas_call(
        _narrow_kernel,
        out_shape=[jax.ShapeDtypeStruct((rows, A_COLS), BF16),
                   jax.ShapeDtypeStruct((rows, b_cols), BF16),
                   jax.ShapeDtypeStruct(wo.shape, BF16),
                   jax.ShapeDtypeStruct((GLR_PAD, wgu.shape[1]), BF16),
                   jax.ShapeDtypeStruct(ws.shape, BF16)],
        compiler_params=pltpu.CompilerParams(vmem_limit_bytes=VMEM_LIMIT),
        name="narrow",
    )(wi.T, wo, wgu, ws)


def _mixer(x, n1w, wa, wb, wgu, bgu, gnw, lnw, lnb, ws, bst, wout, ffn_ws):
    bsz, seq, _ = x.shape
    assert seq % MIX_TILE == 0 and MIX_TILE % MIX_SUB == 0 and MIX_SUB % PAIR == 0
    n_t = seq // MIX_TILE
    consts = (n1w, wa, wb, wgu, bgu, gnw, lnw, lnb, ws, bst, wout)

    def slab_spec(w):
        rows, width = w.shape
        slab = next(r for r in range(BF16_SUBLANES, rows + 1, BF16_SUBLANES)
                    if rows % r == 0 and r * bsz * n_t >= rows)
        last = rows // slab - 1
        return pl.BlockSpec((slab, width), lambda b, t: (jnp.minimum(b * n_t + t, last), 0))

    slab_specs = [slab_spec(w) for w in ffn_ws]
    tile_spec = pl.BlockSpec((1, MIX_TILE, D_MODEL), lambda b, t: (b, t, 0))
    h, *narrowed = pl.pallas_call(
        _mixer_kernel,
        grid=(bsz, n_t),
        in_specs=[tile_spec] + [_const_spec(a.shape) for a in consts] + slab_specs,
        out_specs=[tile_spec] + slab_specs,
        out_shape=[jax.ShapeDtypeStruct(x.shape, F32)]
        + [jax.ShapeDtypeStruct(w.shape, BF16) for w in ffn_ws],
        scratch_shapes=[
            pltpu.VMEM((GLA_KEY, GLA_DV), F32),
            pltpu.VMEM((MIX_TILE, GLA_WIDTH), F32),
            pltpu.VMEM((MIX_TILE, D_MODEL), BF16),
            pltpu.VMEM((D_MODEL, GLA_KEY), BF16),
        ],
        compiler_params=pltpu.CompilerParams(
            dimension_semantics=("arbitrary", "arbitrary"),
            vmem_limit_bytes=VMEM_LIMIT),
        name="mixer",
    )(x, *consts, *ffn_ws)
    return h, narrowed


def _ffn(h, n2w, wg, wu, wd, fnw):
    m = h.shape[0]
    assert m % FFN_TILE == 0
    consts = (n2w, wg, wu, wd, fnw)
    return pl.pallas_call(
        _ffn_kernel,
        grid=(m // FFN_TILE,),
        in_specs=[pl.BlockSpec((FFN_TILE, D_MODEL), lambda i: (i, 0))]
        + [_const_spec(a.shape) for a in consts],
        out_specs=pl.BlockSpec((FFN_TILE, D_MODEL), lambda i: (i, 0)),
        out_shape=jax.ShapeDtypeStruct(h.shape, F32),
        scratch_shapes=[pltpu.VMEM((FFN_TILE, D_FF), BF16)],
        compiler_params=pltpu.CompilerParams(
            dimension_semantics=("parallel",),
            vmem_limit_bytes=VMEM_LIMIT),
        name="ffn",
    )(h, *consts)


def kernel(x, norm1_w, w_in, w_gate_up, b_gate_up, gla_norm_w, sg_ln_w, sg_ln_b, sg_w_s, sg_b_s,
           w_out, norm2_w, w_ffn_gate, w_ffn_up, w_ffn_down, final_norm_w):
    bsz, seq, _ = x.shape
    assert w_in.shape[0] == 1, "the final norm is fused into the (single) layer's ffn call"
    wa, wb, wout, wgu, ws = _narrow_mixer_weights(w_in[0], w_out[0], w_gate_up[0], sg_w_s[0])
    h, (wg, wu, wd) = _mixer(
        x, norm1_w[0][None, :], wa, wb, wgu, b_gate_up[0][None, :], gla_norm_w[0][None, :],
        sg_ln_w[0], sg_ln_b[0], ws, jnp.transpose(sg_b_s[0]),
        wout, (w_ffn_gate[0], w_ffn_up[0], w_ffn_down[0]))
    out = _ffn(h.reshape(bsz * seq, D_MODEL), norm2_w[0][None, :], wg, wu, wd,
               final_norm_w[None, :])
    return out.reshape(bsz, seq, D_MODEL)
```
